```python
import jax
import jax.numpy as jnp
from jax import lax
import numpy as np

D_MODEL = 1024
BATCH = 16
SEQ = 2048
DEPTH = 1

SB_HEAD_DIM = 64
SB_WIDTH = D_MODEL
SB_HEADS = SB_WIDTH // SB_HEAD_DIM
QUERY_BLOCK = 128
ML_HEADS = 4
ML_WIDTH = D_MODEL
ML_HEAD_DIM = ML_WIDTH // ML_HEADS
ML_CHUNK = 64
CONV_WIDTH = 4
N_EXPERTS = 32
TOP_K = 4
D_FF = D_MODEL
SWIGLU_LIMIT = 7.0
SWIGLU_ALPHA = 1.702
MOE_BLOCK = 128
N_MOD = 6
EPS = 1e-6
IN_SPLIT = (SB_WIDTH, SB_WIDTH, SB_WIDTH, 2 * ML_WIDTH, ML_WIDTH, ML_WIDTH, ML_HEADS, ML_HEADS, D_MODEL, D_MODEL)
N_IN = 3 * SB_WIDTH + 4 * ML_WIDTH + 2 * ML_HEADS + 2 * D_MODEL

kernel_name = "hybrid_stickbreak_mlstm_moe_adaln"


def rmsnorm(x, g):
    xf = x.astype(jnp.float32)
    y = xf * lax.rsqrt(jnp.mean(xf * xf, axis=-1, keepdims=True) + EPS)
    return (y * g.astype(jnp.float32)).astype(x.dtype)


def split_heads(t, n_heads):
    b, s, w = t.shape
    return t.reshape(b, s, n_heads, w // n_heads).transpose(0, 2, 1, 3)


def merge_heads(t):
    b, h, s, d = t.shape
    return t.transpose(0, 2, 1, 3).reshape(b, s, h * d)


def stick_breaking_attention(q, k, v):
    seq = q.shape[2]
    scale = q.shape[-1] ** -0.5
    outs = []
    for blk in range(seq // QUERY_BLOCK):
        t0 = blk * QUERY_BLOCK
        t1 = t0 + QUERY_BLOCK
        z = jnp.einsum('bhtd,bhsd->bhts', q[:, :, t0:t1], k[:, :, :t1],
                       preferred_element_type=jnp.float32) * scale
        strict = jnp.arange(t1)[None, :] < jnp.arange(t0, t1)[:, None]
        log_beta = jax.nn.log_sigmoid(z)
        log_keep = jnp.where(strict, jax.nn.log_sigmoid(-z), 0.0)
        between = lax.cumsum(log_keep, axis=3, reverse=True) - log_keep
        a = jnp.where(strict, jnp.exp(log_beta + between), 0.0)
        outs.append(jnp.einsum('bhts,bhsd->bhtd', a, v[:, :, :t1].astype(jnp.float32)))
    return jnp.concatenate(outs, axis=2).astype(v.dtype)


def causal_conv(x, w, b):
    width = w.shape[0]
    seq = x.shape[1]
    xp = jnp.pad(x, ((0, 0), (width - 1, 0), (0, 0)))
    y = b
    for tap in range(width):
        y = y + xp[:, tap:tap + seq] * w[tap]
    return y


def mlstm_chunkwise(q, k, v, i_pre, f_pre):
    f32 = jnp.float32
    bsz, nh, seq, dk = q.shape
    dv = v.shape[-1]
    L = ML_CHUNK
    nc = seq // L
    q = q.astype(f32)
    k = k.astype(f32) * (dk ** -0.5)
    v = v.astype(f32)
    ig = i_pre.astype(f32)
    lf = jax.nn.log_sigmoid(f_pre.astype(f32))

    def to_chunks(t):
        t = t.reshape(bsz, nh, nc, L, *t.shape[3:])
        return jnp.moveaxis(t, 2, 0)

    incl = jnp.tril(jnp.ones((L, L), dtype=bool))

    def step(carry, xs):
        c_mat, n_vec, m_prev = carry
        qc, kc, vc, ic, fc = xs
        b = jnp.cumsum(fc, axis=-1)
        d = jnp.where(incl, b[..., :, None] - b[..., None, :] + ic[..., None, :], -jnp.inf)
        m_row = jnp.maximum(b + m_prev[..., None], jnp.max(d, axis=-1))
        w_inter = jnp.exp(b + m_prev[..., None] - m_row)
        w_intra = jnp.exp(d - m_row[..., None])
        s = jnp.einsum('bhtd,bhsd->bhts', qc, kc) * w_intra
        num = (w_inter[..., None] * jnp.einsum('bhtd,bhde->bhte', qc, c_mat)
               + jnp.einsum('bhts,bhse->bhte', s, vc))
        den = w_inter * jnp.einsum('bhtd,bhd->bht', qc, n_vec) + jnp.sum(s, axis=-1)
        h = num / jnp.maximum(jnp.abs(den), jnp.exp(-m_row))[..., None]
        m_new = m_row[..., -1]
        decay = jnp.exp(b[..., -1] + m_prev - m_new)
        w_state = jnp.exp(b[..., -1:] - b + ic - m_new[..., None])
        wk = w_state[..., None] * kc
        c_new = decay[..., None, None] * c_mat + jnp.einsum('bhsd,bhse->bhde', wk, vc)
        n_new = decay[..., None] * n_vec + jnp.sum(wk, axis=2)
        return (c_new, n_new, m_new), h

    init = (jnp.zeros((bsz, nh, dk, dv), f32), jnp.zeros((bsz, nh, dk), f32), jnp.zeros((bsz, nh), f32))
    _, h = lax.scan(step, init, (to_chunks(q), to_chunks(k), to_chunks(v), to_chunks(ig), to_chunks(lf)))
    return jnp.moveaxis(h, 0, 2).reshape(bsz, nh, seq, dv)


def hybrid_mixer(h, w_in, conv_w, conv_b, b_i, b_f, norm_g, w_a, w_b, w_out):
    proj = h @ w_in
    cuts = np.cumsum(IN_SPLIT)[:-1].tolist()
    sb_q, sb_k, sb_v, ml_qk, ml_v, ml_o, ml_i, ml_f, g_a, g_b = jnp.split(proj, cuts, axis=-1)
    y_a = merge_heads(stick_breaking_attention(split_heads(sb_q, SB_HEADS), split_heads(sb_k, SB_HEADS),
                                               split_heads(sb_v, SB_HEADS)))
    ml_qk = jax.nn.silu(causal_conv(ml_qk, conv_w, conv_b))
    ml_q, ml_k = jnp.split(ml_qk, 2, axis=-1)
    i_pre = (ml_i + b_i).transpose(0, 2, 1)
    f_pre = (ml_f + b_f).transpose(0, 2, 1)
    hb = mlstm_chunkwise(split_heads(ml_q, ML_HEADS), split_heads(ml_k, ML_HEADS),
                         split_heads(ml_v, ML_HEADS), i_pre, f_pre)
    hb = rmsnorm(hb, norm_g.reshape(ML_HEADS, 1, ML_HEAD_DIM))
    y_b = merge_heads(hb).astype(h.dtype) * jax.nn.sigmoid(ml_o)
    merged = jax.nn.sigmoid(g_a) * (y_a @ w_a) + jax.nn.sigmoid(g_b) * (y_b @ w_b)
    return merged @ w_out


def clamped_swiglu_expert(xb, w1, b1, w2, b2):
    a = xb @ w1 + b1
    x_glu, x_lin = jnp.split(a, 2, axis=-1)
    x_glu = jnp.minimum(x_glu, SWIGLU_LIMIT)
    x_lin = jnp.clip(x_lin, -SWIGLU_LIMIT, SWIGLU_LIMIT)
    return ((x_lin + 1.0) * (x_glu * jax.nn.sigmoid(SWIGLU_ALPHA * x_glu))) @ w2 + b2


def routed_moe(h, router_w, router_b, w1, b1, w2, b2):
    bsz, seq, d = h.shape
    n_tok = bsz * seq
    n_assign = n_tok * TOP_K
    xt = h.reshape(n_tok, d)
    logits = (xt @ router_w + router_b).astype(jnp.float32)
    top_logit, top_idx = lax.top_k(logits, TOP_K)
    top_w = jax.nn.softmax(top_logit, axis=-1)
    flat_e = top_idx.reshape(-1)
    flat_tok = jnp.repeat(jnp.arange(n_tok, dtype=jnp.int32), TOP_K)
    flat_w = top_w.reshape(-1)
    order = jnp.argsort(flat_e)
    sorted_e = flat_e[order]
    counts = jnp.bincount(flat_e, length=N_EXPERTS)
    padded = (counts + MOE_BLOCK - 1) // MOE_BLOCK * MOE_BLOCK
    start = jnp.cumsum(counts) - counts
    pad_end = jnp.cumsum(padded)
    pad_start = pad_end - padded
    dest = pad_start[sorted_e] + jnp.arange(n_assign, dtype=jnp.int32) - start[sorted_e]
    n_rows = n_assign + N_EXPERTS * MOE_BLOCK
    n_blocks = n_rows // MOE_BLOCK
    row_tok = jnp.zeros((n_rows,), jnp.int32).at[dest].set(flat_tok[order])
    row_w = jnp.zeros((n_rows,), jnp.float32).at[dest].set(flat_w[order])
    block_e = jnp.minimum(jnp.searchsorted(pad_end, jnp.arange(n_blocks, dtype=jnp.int32) * MOE_BLOCK,
                                           side='right'), N_EXPERTS - 1)
    xs = xt[row_tok].reshape(n_blocks, MOE_BLOCK, d)

    def run_block(args):
        xb, e = args
        return clamped_swiglu_expert(xb, w1[e], b1[e], w2[e], b2[e])

    rows = lax.map(run_block, (xs, block_e)).reshape(n_rows, d)
    y = jnp.zeros((n_tok, d), h.dtype).at[row_tok].add((rows * row_w[:, None]).astype(h.dtype))
    return y.reshape(bsz, seq, d)


def setup_inputs(seed: int = 0) -> dict:
    key = jax.random.key(seed)
    ks = jax.random.split(key, 24)
    f32 = jnp.float32
    nrm = lambda k, shape, s: jax.random.normal(k, shape, f32) * s
    return {
        "x": nrm(ks[0], (BATCH, SEQ, D_MODEL), 1.0),
        "c": nrm(ks[1], (BATCH, D_MODEL), 1.0),
        "ada_w": nrm(ks[2], (DEPTH, D_MODEL, N_MOD * D_MODEL), 0.2 * D_MODEL ** -0.5),
        "ada_b": nrm(ks[3], (DEPTH, N_MOD * D_MODEL), 0.02),
        "norm1_g": 1.0 + nrm(ks[4], (DEPTH, D_MODEL), 0.02),
        "w_in": nrm(ks[5], (DEPTH, D_MODEL, N_IN), D_MODEL ** -0.5),
        "conv_w": nrm(ks[6], (DEPTH, CONV_WIDTH, 2 * ML_WIDTH), CONV_WIDTH ** -0.5),
        "conv_b": nrm(ks[7], (DEPTH, 2 * ML_WIDTH), 0.02),
        "ml_b_i": nrm(ks[8], (DEPTH, ML_HEADS), 0.1),
        "ml_b_f": jnp.broadcast_to(jnp.linspace(3.0, 6.0, ML_HEADS, dtype=f32), (DEPTH, ML_HEADS))
                   + nrm(ks[9], (DEPTH, ML_HEADS), 0.1),
        "ml_norm_g": 1.0 + nrm(ks[10], (DEPTH, ML_WIDTH), 0.02),
        "w_branch_a": nrm(ks[11], (DEPTH, SB_WIDTH, D_MODEL), SB_WIDTH ** -0.5),
        "w_branch_b": nrm(ks[12], (DEPTH, ML_WIDTH, D_MODEL), ML_WIDTH ** -0.5),
        "w_out": nrm(ks[13], (DEPTH, D_MODEL, D_MODEL), D_MODEL ** -0.5),
        "norm2_g": 1.0 + nrm(ks[14], (DEPTH, D_MODEL), 0.02),
        "router_w": nrm(ks[15], (DEPTH, D_MODEL, N_EXPERTS), D_MODEL ** -0.5),
        "router_b": nrm(ks[16], (DEPTH, N_EXPERTS), 0.01),
        "expert_w1": nrm(ks[17], (DEPTH, N_EXPERTS, D_MODEL, 2 * D_FF), D_MODEL ** -0.5),
        "expert_b1": nrm(ks[18], (DEPTH, N_EXPERTS, 2 * D_FF), 0.01),
        "expert_w2": nrm(ks[19], (DEPTH, N_EXPERTS, D_FF, D_MODEL), D_FF ** -0.5),
        "expert_b2": nrm(ks[20], (DEPTH, N_EXPERTS, D_MODEL), 0.01),
        "final_g": 1.0 + nrm(ks[21], (D_MODEL,), 0.02),
    }


def reference(x, c, ada_w, ada_b, norm1_g, w_in, conv_w, conv_b, ml_b_i, ml_b_f, ml_norm_g,
              w_branch_a, w_branch_b, w_out, norm2_g, router_w, router_b,
              expert_w1, expert_b1, expert_w2, expert_b2, final_g):
    for layer in range(DEPTH):
        mod = (jax.nn.silu(c) @ ada_w[layer] + ada_b[layer])[:, None, :]
        sh1, sc1, g1, sh2, sc2, g2 = jnp.split(mod, N_MOD, axis=-1)
        h = rmsnorm(x, norm1_g[layer]) * (1.0 + sc1) + sh1
        x = x + g1 * hybrid_mixer(h, w_in[layer], conv_w[layer], conv_b[layer], ml_b_i[layer], ml_b_f[layer],
                                  ml_norm_g[layer], w_branch_a[layer], w_branch_b[layer], w_out[layer])
        h = rmsnorm(x, norm2_g[layer]) * (1.0 + sc2) + sh2
        x = x + g2 * routed_moe(h, router_w[layer], router_b[layer], expert_w1[layer], expert_b1[layer],
                                expert_w2[layer], expert_b2[layer])
    return rmsnorm(x, final_g)
```

```python
import functools

import jax
import jax.numpy as jnp
from jax import lax
from jax.experimental import pallas as pl
from jax.experimental.pallas import tpu as pltpu

F32 = jnp.float32
BF16 = jnp.bfloat16
I32 = jnp.int32
HIGHEST = lax.Precision.HIGHEST

D_MODEL = 1024
SB_HEAD_DIM = 64
ML_HEADS = 4
ML_HEAD_DIM = 256
CONV_WIDTH = 4
N_EXPERTS = 32
TOP_K = 4
D_FF = 1024
SWIGLU_LIMIT = 7.0
SWIGLU_ALPHA = 1.702
N_MOD = 6
EPS = 1e-6

LANES = 128
SUBLANES = 8
SLAB = D_MODEL // LANES

QUERY_TILE = 128
ML_CHUNK = 256
ROW_TILE_PROJ = 1024
ROW_TILE_MERGE = 256
ROW_TILE_MOE = 256
VMEM_LIMIT = 56 * 1024 * 1024


def _nt(a, b, precision=None):
    return lax.dot_general(a, b, (((1,), (1,)), ((), ())), preferred_element_type=F32, precision=precision)


def _mm(a, b, precision=None):
    return jnp.dot(a, b, preferred_element_type=F32, precision=precision)


def _sigmoid(x):
    return 1.0 / (1.0 + jnp.exp(-x))


def _log_sigmoid(x):
    return jnp.minimum(x, 0.0) - jnp.log(1.0 + jnp.exp(-jnp.abs(x)))


def _ada_kernel(c_ref, w_ref, b_ref, o_ref):
    c = c_ref[...]
    o_ref[...] = _mm(c * _sigmoid(c), w_ref[...], precision=HIGHEST) + b_ref[...]


def _ada(c, ada_w, ada_b):
    bsz = c.shape[0]
    n = ada_w.shape[1]
    return pl.pallas_call(
        _ada_kernel,
        grid=(n // D_MODEL,),
        in_specs=[
            pl.BlockSpec((bsz, D_MODEL), lambda j: (0, 0)),
            pl.BlockSpec((D_MODEL, D_MODEL), lambda j: (0, j)),
            pl.BlockSpec((1, D_MODEL), lambda j: (0, j)),
        ],
        out_specs=pl.BlockSpec((bsz, D_MODEL), lambda j: (0, j)),
        out_shape=jax.ShapeDtypeStruct((bsz, n), F32),
        name="ada",
    )(c, ada_w, ada_b.reshape(1, n))


def _proj_kernel(x_ref, mod_ref, g_ref, w_ref, wg_ref, o_ref, og_ref, h_ref):
    @pl.when(pl.program_id(1) == 0)
    def _():
        x = x_ref[...]
        y = x * lax.rsqrt(jnp.mean(x * x, axis=-1, keepdims=True) + EPS) * g_ref[...]
        m = mod_ref[0]
        hb = (y * (1.0 + m[1:2]) + m[0:1]).astype(BF16)
        h_ref[...] = hb
        og_ref[...] = _mm(hb, wg_ref[...])

    o_ref[...] = _mm(h_ref[...], w_ref[...]).astype(BF16)


def _proj(x2, mod3, norm_g, w_main, w_gate, seq):
    n_tok = x2.shape[0]
    n_main = w_main.shape[1]
    tm = min(ROW_TILE_PROJ, seq)
    tn = D_MODEL
    return pl.pallas_call(
        _proj_kernel,
        grid=(n_tok // tm, n_main // tn),
        in_specs=[
            pl.BlockSpec((tm, D_MODEL), lambda i, j: (i, 0)),
            pl.BlockSpec((1, N_MOD, D_MODEL), lambda i, j: ((i * tm) // seq, 0, 0)),
            pl.BlockSpec((1, D_MODEL), lambda i, j: (0, 0)),
            pl.BlockSpec((D_MODEL, tn), lambda i, j: (0, j)),
            pl.BlockSpec((D_MODEL, LANES), lambda i, j: (0, 0)),
        ],
        out_specs=[
            pl.BlockSpec((tm, tn), lambda i, j: (i, j)),
            pl.BlockSpec((tm, LANES), lambda i, j: (i, 0)),
        ],
        out_shape=[
            jax.ShapeDtypeStruct((n_tok, n_main), BF16),
            jax.ShapeDtypeStruct((n_tok, LANES), F32),
        ],
        scratch_shapes=[pltpu.VMEM((tm, D_MODEL), BF16)],
        compiler_params=pltpu.CompilerParams(
            dimension_semantics=("arbitrary", "arbitrary"), vmem_limit_bytes=VMEM_LIMIT),
        name="proj",
    )(x2, mod3, norm_g.reshape(1, D_MODEL), w_main, w_gate)


def _sb_kernel(q_ref, k_ref, v_ref, uu_ref, o_ref):
    seq = q_ref.shape[0]
    qt = QUERY_TILE
    lane = lax.broadcasted_iota(I32, (qt, LANES), 1)
    row = lax.broadcasted_iota(I32, (qt, LANES), 0)
    strict = lane < row
    uu = uu_ref[...]

    def tile(qh, k, v, run, acc, masked):
        z = _nt(qh, k)
        nz = -z
        lk = jnp.minimum(nz, 0.0) - jnp.log(1.0 + jnp.exp(jnp.minimum(z, nz)))
        if masked:
            lk = jnp.where(strict, lk, 0.0)
        hi = lk.astype(BF16)
        lo = (lk - hi.astype(F32)).astype(BF16)
        cc = _mm(jnp.concatenate([hi, lo], axis=1), uu)
        a = jnp.exp(z + cc[:, :LANES] + run)
        if masked:
            a = jnp.where(strict, a, 0.0)
        acc = acc + _mm(a.astype(BF16), v)
        return run + cc[:, LANES:], acc

    def qblock(i, carry):
        r0 = pl.multiple_of(i * qt, qt)
        qf = q_ref[pl.ds(r0, qt), :].astype(F32) * (SB_HEAD_DIM ** -0.5)
        kd = k_ref[pl.ds(r0, qt), :]
        vd = v_ref[pl.ds(r0, qt), :]
        outs = []
        for h in range(LANES // SB_HEAD_DIM):
            in_head = (lane >= h * SB_HEAD_DIM) & (lane < (h + 1) * SB_HEAD_DIM)
            qh = jnp.where(in_head, qf, 0.0).astype(BF16)
            zeros = jnp.zeros((qt, LANES), F32)
            run, acc = tile(qh, kd, vd, zeros, zeros, True)

            def kblock(j, rc, qh=qh):
                s0 = pl.multiple_of((i - 1 - j) * qt, qt)
                return tile(qh, k_ref[pl.ds(s0, qt), :], v_ref[pl.ds(s0, qt), :], rc[0], rc[1], False)

            run, acc = lax.fori_loop(0, i, kblock, (run, acc))
            outs.append(acc)
        o_ref[pl.ds(r0, qt), :] = jnp.where(lane < SB_HEAD_DIM, outs[0], outs[1]).astype(BF16)
        return carry

    lax.fori_loop(0, seq // qt, qblock, 0)


def _sb_attention(proj, uu, bsz, seq):
    n_pairs = D_MODEL // LANES
    return pl.pallas_call(
        _sb_kernel,
        grid=(bsz, n_pairs),
        in_specs=[
            pl.BlockSpec((seq, LANES), lambda b, p: (b, p)),
            pl.BlockSpec((seq, LANES), lambda b, p: (b, n_pairs + p)),
            pl.BlockSpec((seq, LANES), lambda b, p: (b, 2 * n_pairs + p)),
            pl.BlockSpec((2 * LANES, 2 * LANES), lambda b, p: (0, 0)),
        ],
        out_specs=pl.BlockSpec((seq, LANES), lambda b, p: (b, p)),
        out_shape=jax.ShapeDtypeStruct((bsz * seq, D_MODEL), BF16),
        compiler_params=pltpu.CompilerParams(dimension_semantics=("arbitrary", "arbitrary")),
        name="sb_attn",
    )(proj, proj, proj, uu)


def _mlstm_kernel(q_ref, k_ref, v_ref, og_ref, gt_ref, gb_ref, cwq_ref, cwk_ref, cbq_ref, cbk_ref, ng_ref, tri_ref,
                  y_ref, xq_s, xk_s, c_s, n_s, m_s):
    head = pl.program_id(1)
    seq = q_ref.shape[0]
    cl = min(ML_CHUNK, seq)
    pad = SUBLANES
    xq_s[0:pad, :] = jnp.zeros((pad, ML_HEAD_DIM), F32)
    xk_s[0:pad, :] = jnp.zeros((pad, ML_HEAD_DIM), F32)
    c_s[...] = jnp.zeros_like(c_s)
    n_s[...] = jnp.zeros_like(n_s)
    m_s[...] = jnp.zeros_like(m_s)
    lane = lax.broadcasted_iota(I32, (cl, LANES), 1)
    sel_i = (lane == head).astype(F32)
    sel_f = (lane == head + ML_HEADS).astype(F32)
    tril = lax.broadcasted_iota(I32, (cl, cl), 1) <= lax.broadcasted_iota(I32, (cl, cl), 0)

    def conv_silu(x_s, cw_ref, cb_ref):
        y = cb_ref[...]
        for tap in range(CONV_WIDTH):
            lo = pad - (CONV_WIDTH - 1) + tap
            y = y + x_s[lo:lo + cl, :] * cw_ref[tap:tap + 1, :]
        x_s[0:pad, :] = x_s[cl:cl + pad, :]
        return y * _sigmoid(y)

    def chunk(c, carry):
        r0 = pl.multiple_of(c * cl, cl)
        xq_s[pad:pad + cl, :] = q_ref[pl.ds(r0, cl), :].astype(F32)
        xk_s[pad:pad + cl, :] = k_ref[pl.ds(r0, cl), :].astype(F32)
        qc = conv_silu(xq_s, cwq_ref, cbq_ref)
        kc = conv_silu(xk_s, cwk_ref, cbk_ref) * (ML_HEAD_DIM ** -0.5)
        qb = qc.astype(BF16)
        kb = kc.astype(BF16)
        vb = v_ref[pl.ds(r0, cl), :]

        pre = gt_ref[pl.ds(r0, cl), :] + gb_ref[...]
        bt = _mm(tri_ref[...], _log_sigmoid(pre), precision=HIGHEST)
        b_col = jnp.sum(bt * sel_f, axis=1, keepdims=True)
        i_col = jnp.sum(pre * sel_i, axis=1, keepdims=True)
        g_col = i_col - b_col
        g_row = jnp.transpose(jnp.broadcast_to(g_col, (cl, LANES)))[0:1, :]

        m_prev = m_s[...]
        dmat = jnp.where(tril, b_col + g_row, -jnp.inf)
        m_row = jnp.maximum(b_col + m_prev, jnp.max(dmat, axis=1, keepdims=True))
        w_inter = jnp.exp(b_col + m_prev - m_row)
        s_mat = _nt(qb, kb) * jnp.exp(dmat - m_row)
        num = w_inter * _mm(qb, c_s[...].astype(BF16)) + _mm(s_mat.astype(BF16), vb)
        den = w_inter * jnp.sum(qc * n_s[...], axis=1, keepdims=True) + jnp.sum(s_mat, axis=1, keepdims=True)
        hh = num * (1.0 / jnp.maximum(jnp.abs(den), jnp.exp(-m_row)))

        m_new = m_row[cl - 1:cl, :]
        b_last = b_col[cl - 1:cl, :]
        decay = jnp.exp(b_last + m_prev - m_new)
        wk = jnp.exp(b_last + g_col - m_new) * kc
        c_s[...] = decay * c_s[...] + lax.dot_general(
            wk.astype(BF16), vb, (((0,), (0,)), ((), ())), preferred_element_type=F32)
        n_s[...] = decay * n_s[...] + jnp.sum(wk, axis=0, keepdims=True)
        m_s[...] = m_new

        hn = hh * lax.rsqrt(jnp.mean(hh * hh, axis=1, keepdims=True) + EPS) * ng_ref[...]
        y_ref[pl.ds(r0, cl), :] = (hn * _sigmoid(og_ref[pl.ds(r0, cl), :].astype(F32))).astype(BF16)
        return carry

    lax.fori_loop(0, seq // cl, chunk, 0)


def _mlstm(proj, gates, gate_bias, conv_w, conv_b, norm_g, tri, bsz, seq):
    hd = ML_HEAD_DIM
    cl = min(ML_CHUNK, seq)
    col0 = 3 * D_MODEL // hd
    row_spec = lambda off: pl.BlockSpec((seq, hd), lambda b, h: (b, off + h))
    return pl.pallas_call(
        _mlstm_kernel,
        grid=(bsz, ML_HEADS),
        in_specs=[
            row_spec(col0), row_spec(col0 + ML_HEADS), row_spec(col0 + 2 * ML_HEADS), row_spec(col0 + 3 * ML_HEADS),
            pl.BlockSpec((seq, LANES), lambda b, h: (b, 0)),
            pl.BlockSpec((1, LANES), lambda b, h: (0, 0)),
            pl.BlockSpec((CONV_WIDTH, hd), lambda b, h: (0, h)),
            pl.BlockSpec((CONV_WIDTH, hd), lambda b, h: (0, ML_HEADS + h)),
            pl.BlockSpec((1, hd), lambda b, h: (0, h)),
            pl.BlockSpec((1, hd), lambda b, h: (0, ML_HEADS + h)),
            pl.BlockSpec((1, hd), lambda b, h: (0, h)),
            pl.BlockSpec((cl, cl), lambda b, h: (0, 0)),
        ],
        out_specs=pl.BlockSpec((seq, hd), lambda b, h: (b, h)),
        out_shape=jax.ShapeDtypeStruct((bsz * seq, D_MODEL), BF16),
        scratch_shapes=[
            pltpu.VMEM((cl + 2 * SUBLANES, hd), F32),
            pltpu.VMEM((cl + 2 * SUBLANES, hd), F32),
            pltpu.VMEM((hd, hd), F32),
            pltpu.VMEM((1, hd), F32),
            pltpu.VMEM((1, 1), F32),
        ],
        compiler_params=pltpu.CompilerParams(
            dimension_semantics=("arbitrary", "arbitrary"), vmem_limit_bytes=VMEM_LIMIT),
        name="mlstm",
    )(proj, proj, proj, proj, gates, gate_bias, conv_w, conv_w,
      conv_b.reshape(1, -1), conv_b.reshape(1, -1), norm_g.reshape(1, -1), tri)


def _merge_kernel(ya_ref, yb_ref, ga_ref, gb_ref, x_ref, mod_ref, wa_ref, wb_ref, wo_ref, n2_ref, rwt_ref, rb_ref,
                  su_ref, x1_ref, h2_ref, idx_ref, wt_ref, rank_ref, cnt_ref, carry_s):
    @pl.when(pl.program_id(0) == 0)
    def _():
        carry_s[...] = jnp.zeros_like(carry_s)

    tm = x_ref.shape[0]
    a = _mm(ya_ref[...], wa_ref[...])
    b = _mm(yb_ref[...], wb_ref[...])
    merged = _sigmoid(ga_ref[...].astype(F32)) * a + _sigmoid(gb_ref[...].astype(F32)) * b
    m = mod_ref[0]
    x1 = x_ref[...] + m[2:3] * _mm(merged.astype(BF16), wo_ref[...])
    x1_ref[...] = x1
    y = x1 * lax.rsqrt(jnp.mean(x1 * x1, axis=-1, keepdims=True) + EPS) * n2_ref[...]
    h2 = y * (1.0 + m[4:5]) + m[3:4]
    for s in range(SLAB):
        h2_ref[:, s, :] = h2[:, s * LANES:(s + 1) * LANES]

    logits = _nt(rwt_ref[...], h2, precision=HIGHEST) + rb_ref[...]
    ie = lax.broadcasted_iota(I32, (N_EXPERTS, tm), 0).astype(F32)
    idxs, vals = [], []
    for _ in range(TOP_K):
        mx = jnp.max(logits, axis=0, keepdims=True)
        am = jnp.min(jnp.where(logits == mx, ie, float(N_EXPERTS)), axis=0, keepdims=True)
        idxs.append(am)
        vals.append(mx)
        logits = jnp.where(ie == am, -jnp.inf, logits)
    exps = [jnp.exp(v - vals[0]) for v in vals]
    inv = 1.0 / (exps[0] + exps[1] + exps[2] + exps[3])
    wt_ref[...] = jnp.concatenate([e * inv for e in exps], axis=0)
    idx_ref[...] = jnp.concatenate(idxs, axis=0).astype(I32)

    onehot = jnp.zeros((N_EXPERTS, tm), F32)
    for am in idxs:
        onehot = onehot + (ie == am).astype(F32)
    before = _mm(onehot.astype(BF16), su_ref[...]) + carry_s[:, 0:1]
    ranks = [jnp.sum(jnp.where(ie == am, before, 0.0), axis=0, keepdims=True) for am in idxs]
    rank_ref[...] = jnp.concatenate(ranks, axis=0).astype(I32)
    carry_s[...] = carry_s[...] + jnp.sum(onehot, axis=1, keepdims=True)
    cnt_ref[...] = carry_s[...]


def _merge(ya, yb, proj, x2, mod3, w_a, w_b, w_o, norm2_g, router_wt, router_b, su, seq):
    n_tok = x2.shape[0]
    tm = min(ROW_TILE_MERGE, seq)
    d = D_MODEL
    gcol = 7 * d // d
    row = lambda j: pl.BlockSpec((tm, d), lambda i: (i, j))
    const = lambda shape: pl.BlockSpec(shape, lambda i: tuple(0 for _ in shape))
    tok4 = pl.BlockSpec((TOP_K, tm), lambda i: (0, i))
    return pl.pallas_call(
        _merge_kernel,
        grid=(n_tok // tm,),
        in_specs=[
            row(0), row(0), row(gcol), row(gcol + 1), row(0),
            pl.BlockSpec((1, N_MOD, d), lambda i: ((i * tm) // seq, 0, 0)),
            const((d, d)), const((d, d)), const((d, d)), const((1, d)),
            const((N_EXPERTS, d)), const((N_EXPERTS, 1)), const((tm, tm)),
        ],
        out_specs=[
            row(0),
            pl.BlockSpec((tm, SLAB, LANES), lambda i: (i, 0, 0)),
            tok4, tok4, tok4,
            const((N_EXPERTS, LANES)),
        ],
        out_shape=[
            jax.ShapeDtypeStruct((n_tok, d), F32),
            jax.ShapeDtypeStruct((n_tok, SLAB, LANES), F32),
            jax.ShapeDtypeStruct((TOP_K, n_tok), I32),
            jax.ShapeDtypeStruct((TOP_K, n_tok), F32),
            jax.ShapeDtypeStruct((TOP_K, n_tok), I32),
            jax.ShapeDtypeStruct((N_EXPERTS, LANES), F32),
        ],
        scratch_shapes=[pltpu.VMEM((N_EXPERTS, LANES), F32)],
        compiler_params=pltpu.CompilerParams(dimension_semantics=("arbitrary",), vmem_limit_bytes=VMEM_LIMIT),
        name="merge",
    )(ya, yb, proj, proj, x2, mod3, w_a, w_b, w_o, norm2_g.reshape(1, d), router_wt, router_b.reshape(-1, 1), su)


def _scatter_kernel(start_ref, fill_lo_ref, fill_hi_ref, nb_ref, idx_ref, rank_ref, h2_ref, xs_ref, zero_s, sem, zsem):
    tm = h2_ref.shape[0]
    bm = zero_s.shape[0]

    @pl.when(pl.program_id(0) == 0)
    def _():
        zero_s[...] = jnp.zeros_like(zero_s)

        def per_expert(e, total):
            lo, hi = fill_lo_ref[e], fill_hi_ref[e]

            def fill(r, c):
                pltpu.make_async_copy(zero_s.at[0], xs_ref.at[r], sem).start()
                return c

            lax.fori_loop(lo, hi, fill, 0)
            return total + (hi - lo)

        total = lax.fori_loop(0, N_EXPERTS, per_expert, 0)

        def drain(r, c):
            pltpu.make_async_copy(zero_s.at[0], xs_ref.at[0], sem).wait()
            return c

        lax.fori_loop(0, total, drain, 0)

        n_blocks = xs_ref.shape[0] // bm

        def fill_block(j, c):
            pltpu.make_async_copy(zero_s, xs_ref.at[pl.ds(pl.multiple_of(j * bm, bm), bm)], zsem).start()
            return c

        lax.fori_loop(nb_ref[0], n_blocks, fill_block, 0)

        def drain_block(j, c):
            pltpu.make_async_copy(zero_s, xs_ref.at[pl.ds(0, bm)], zsem).wait()
            return c

        lax.fori_loop(nb_ref[0], n_blocks, drain_block, 0)

    def send(t, c):
        for r in range(TOP_K):
            dst = start_ref[idx_ref[r, t]] + rank_ref[r, t]
            pltpu.make_async_copy(h2_ref.at[t], xs_ref.at[dst], sem).start()
        return c

    lax.fori_loop(0, tm, send, 0)

    def drain(t, c):
        for r in range(TOP_K):
            pltpu.make_async_copy(h2_ref.at[0], xs_ref.at[0], sem).wait()
        return c

    lax.fori_loop(0, tm, drain, 0)


def _scatter(start, fill_lo, fill_hi, n_used, idx, rank, h2s, n_rows, seq):
    n_tok = h2s.shape[0]
    tm = min(ROW_TILE_MOE, seq)
    smem = lambda: pl.BlockSpec(memory_space=pltpu.SMEM)
    tok4 = pl.BlockSpec((TOP_K, tm), lambda i: (0, i), memory_space=pltpu.SMEM)
    return pl.pallas_call(
        _scatter_kernel,
        grid=(n_tok // tm,),
        in_specs=[smem(), smem(), smem(), smem(), tok4, tok4, pl.BlockSpec((tm, SLAB, LANES), lambda i: (i, 0, 0))],
        out_specs=pl.BlockSpec(memory_space=pl.ANY),
        out_shape=jax.ShapeDtypeStruct((n_rows, SLAB, LANES), F32),
        scratch_shapes=[pltpu.VMEM((ROW_TILE_MOE, SLAB, LANES), F32), pltpu.SemaphoreType.DMA,
                        pltpu.SemaphoreType.DMA],
        compiler_params=pltpu.CompilerParams(dimension_semantics=("arbitrary",)),
        name="scatter",
    )(start, fill_lo, fill_hi, n_used, idx, rank, h2s)


def _expert_kernel(be_ref, nb_ref, xs_ref, w1_ref, b1_ref, w2_ref, b2_ref, ys_ref, w1_s, w2_s):
    j = pl.program_id(0)
    changed = jnp.logical_or(j == 0, be_ref[j] != be_ref[jnp.maximum(j - 1, 0)])

    @pl.when(changed)
    def _():
        w1_s[...] = w1_ref[...].astype(BF16)
        w2_s[...] = w2_ref[...].astype(BF16)

    @pl.when(j < nb_ref[0])
    def _():
        x = jnp.concatenate([xs_ref[:, s, :] for s in range(SLAB)], axis=1).astype(BF16)
        a = _mm(x, w1_s[...]) + b1_ref[...]
        glu = jnp.minimum(a[:, :D_FF], SWIGLU_LIMIT)
        lin = jnp.clip(a[:, D_FF:], -SWIGLU_LIMIT, SWIGLU_LIMIT)
        hmid = (lin + 1.0) * (glu * _sigmoid(SWIGLU_ALPHA * glu))
        y = _mm(hmid.astype(BF16), w2_s[...]) + b2_ref[...]
        for s in range(SLAB):
            ys_ref[:, s, :] = y[:, s * LANES:(s + 1) * LANES]

    @pl.when(j >= nb_ref[0])
    def _():
        ys_ref[...] = jnp.zeros_like(ys_ref)


def _experts(block_e, n_blocks_used, xs, w1, b1, w2, b2):
    n_rows = xs.shape[0]
    bm = ROW_TILE_MOE
    d = D_MODEL
    blk = lambda j, be, nb: (j, 0, 0)
    exp = lambda j, be, nb: (be[j], 0, 0)
    return pl.pallas_call(
        _expert_kernel,
        grid_spec=pltpu.PrefetchScalarGridSpec(
            num_scalar_prefetch=2,
            grid=(n_rows // bm,),
            in_specs=[
                pl.BlockSpec((bm, SLAB, LANES), blk),
                pl.BlockSpec((None, d, 2 * D_FF), exp),
                pl.BlockSpec((None, 1, 2 * D_FF), exp),
                pl.BlockSpec((None, D_FF, d), exp),
                pl.BlockSpec((None, 1, d), exp),
            ],
            out_specs=pl.BlockSpec((bm, SLAB, LANES), blk),
            scratch_shapes=[pltpu.VMEM((d, 2 * D_FF), BF16), pltpu.VMEM((D_FF, d), BF16)],
        ),
        out_shape=jax.ShapeDtypeStruct((n_rows, SLAB, LANES), F32),
        compiler_params=pltpu.CompilerParams(dimension_semantics=("arbitrary",), vmem_limit_bytes=VMEM_LIMIT),
        name="experts",
    )(block_e, n_blocks_used, xs, w1, b1.reshape(N_EXPERTS, 1, -1), w2, b2.reshape(N_EXPERTS, 1, -1))


def _combine_kernel(start_ref, idx_ref, rank_ref, wt_ref, ys_ref, x1_ref, mod_ref, fg_ref, o_ref, buf_s, moe_s, sem):
    tm = x1_ref.shape[0]

    def fetch(t, c):
        for r in range(TOP_K):
            src = start_ref[idx_ref[r, t]] + rank_ref[r, t]
            pltpu.make_async_copy(ys_ref.at[src], buf_s.at[r, t], sem).start()
        return c

    lax.fori_loop(0, tm, fetch, 0)

    def drain(t, c):
        for r in range(TOP_K):
            pltpu.make_async_copy(ys_ref.at[0], buf_s.at[0, 0], sem).wait()
        return c

    lax.fori_loop(0, tm, drain, 0)

    def mix(t, c):
        acc = wt_ref[0, t] * buf_s[0, t]
        for r in range(1, TOP_K):
            acc = acc + wt_ref[r, t] * buf_s[r, t]
        moe_s[t] = acc
        return c

    lax.fori_loop(0, tm, mix, 0)

    moe = jnp.concatenate([moe_s[:, s, :] for s in range(SLAB)], axis=1)
    x2 = x1_ref[...] + mod_ref[0][5:6] * moe
    o_ref[...] = x2 * lax.rsqrt(jnp.mean(x2 * x2, axis=-1, keepdims=True) + EPS) * fg_ref[...]


def _combine(start, idx, rank, wt, ys, x1, mod3, final_g, seq):
    n_tok = x1.shape[0]
    tm = min(ROW_TILE_MOE, seq)
    d = D_MODEL
    tok4 = pl.BlockSpec((TOP_K, tm), lambda i: (0, i), memory_space=pltpu.SMEM)
    return pl.pallas_call(
        _combine_kernel,
        grid=(n_tok // tm,),
        in_specs=[
            pl.BlockSpec(memory_space=pltpu.SMEM), tok4, tok4, tok4,
            pl.BlockSpec(memory_space=pl.ANY),
            pl.BlockSpec((tm, d), lambda i: (i, 0)),
            pl.BlockSpec((1, N_MOD, d), lambda i: ((i * tm) // seq, 0, 0)),
            pl.BlockSpec((1, d), lambda i: (0, 0)),
        ],
        out_specs=pl.BlockSpec((tm, d), lambda i: (i, 0)),
        out_shape=jax.ShapeDtypeStruct((n_tok, d), F32),
        scratch_shapes=[
            pltpu.VMEM((TOP_K, tm, SLAB, LANES), F32),
            pltpu.VMEM((tm, SLAB, LANES), F32),
            pltpu.SemaphoreType.DMA,
        ],
        compiler_params=pltpu.CompilerParams(dimension_semantics=("arbitrary",), vmem_limit_bytes=VMEM_LIMIT),
        name="combine",
    )(start, idx, rank, wt, ys, x1, mod3, final_g.reshape(1, d))


def _layer(x2, mod3, bsz, seq, norm1_g, w_in, conv_w, conv_b, ml_b_i, ml_b_f, ml_norm_g, w_branch_a, w_branch_b,
           w_out, norm2_g, router_w, router_b, expert_w1, expert_b1, expert_w2, expert_b2):
    d = D_MODEL
    n_tok = bsz * seq
    gate0 = 7 * d
    n_gate = 2 * ML_HEADS
    w_main = jnp.concatenate([w_in[:, :gate0], w_in[:, gate0 + n_gate:]], axis=1).astype(BF16)
    w_gate = jnp.pad(w_in[:, gate0:gate0 + n_gate], ((0, 0), (0, LANES - n_gate))).astype(BF16)
    gate_bias = jnp.pad(jnp.concatenate([ml_b_i, ml_b_f]), (0, LANES - n_gate)).reshape(1, LANES)

    proj, gates = _proj(x2, mod3, norm1_g, w_main, w_gate, seq)

    ones = jnp.ones((LANES, LANES), F32)
    u_half = jnp.concatenate([jnp.tril(ones), ones], axis=1)
    uu = jnp.concatenate([u_half, u_half], axis=0).astype(BF16)
    ya = _sb_attention(proj, uu, bsz, seq)

    cl = min(ML_CHUNK, seq)
    tri = jnp.tril(jnp.ones((cl, cl), F32))
    yb = _mlstm(proj, gates, gate_bias, conv_w, conv_b, ml_norm_g, tri, bsz, seq)

    tm = min(ROW_TILE_MERGE, seq)
    su = jnp.triu(jnp.ones((tm, tm), F32), k=1).astype(BF16)
    x1, h2s, idx, wt, rank, cnt = _merge(
        ya, yb, proj, x2, mod3, w_branch_a.astype(BF16), w_branch_b.astype(BF16), w_out.astype(BF16),
        norm2_g, router_w.T, router_b, su, seq)

    bm = ROW_TILE_MOE
    counts = cnt[:, 0].astype(I32)
    padded = (counts + bm - 1) // bm * bm
    pad_end = jnp.cumsum(padded)
    start = pad_end - padded
    n_rows = (n_tok * TOP_K + N_EXPERTS * (bm - 1)) // bm * bm
    n_blocks = n_rows // bm
    n_used = jnp.maximum(pad_end[-1] // bm, 1).astype(I32)
    blk = jnp.minimum(jnp.arange(n_blocks, dtype=I32), n_used - 1)
    block_e = jnp.minimum(jnp.searchsorted(pad_end, blk * bm, side="right"), N_EXPERTS - 1).astype(I32)

    xs = _scatter(start, start + counts, pad_end, n_used.reshape(1), idx, rank, h2s, n_rows, seq)
    ys = _experts(block_e, n_used.reshape(1), xs, expert_w1, expert_b1, expert_w2, expert_b2)
    return start, idx, rank, wt, ys, x1


def kernel(x, c, ada_w, ada_b, norm1_g, w_in, conv_w, conv_b, ml_b_i, ml_b_f, ml_norm_g, w_branch_a, w_branch_b, w_out, norm2_g, router_w, router_b, expert_w1, expert_b1, expert_w2, expert_b2, final_g):
    bsz, seq, d = x.shape
    depth = ada_w.shape[0]
    assert d == D_MODEL and depth == 1 and seq % QUERY_TILE == 0
    x2 = x.reshape(bsz * seq, d)
    mod3 = _ada(c, ada_w[0], ada_b[0]).reshape(bsz, N_MOD, d)
    start, idx, rank, wt, ys, x1 = _layer(
        x2, mod3, bsz, seq, norm1_g[0], w_in[0], conv_w[0], conv_b[0], ml_b_i[0], ml_b_f[0], ml_norm_g[0],
        w_branch_a[0], w_branch_b[0], w_out[0], norm2_g[0], router_w[0], router_b[0],
        expert_w1[0], expert_b1[0], expert_w2[0], expert_b2[0])
    out = _combine(start, idx, rank, wt, ys, x1, mod3, final_g, seq)
    return out.reshape(bsz, seq, d)
```

```python
import functools

import jax
import jax.numpy as jnp
from jax import lax
from jax.experimental import pallas as pl
from jax.experimental.pallas import tpu as pltpu

F32 = jnp.float32
BF16 = jnp.bfloat16
I32 = jnp.int32
HIGHEST = lax.Precision.HIGHEST

D_MODEL = 1024
SB_HEAD_DIM = 64
ML_HEADS = 4
ML_HEAD_DIM = 256
CONV_WIDTH = 4
N_EXPERTS = 32
TOP_K = 4
D_FF = 1024
SWIGLU_LIMIT = 7.0
SWIGLU_ALPHA = 1.702
N_MOD = 6
EPS = 1e-6
LOG2E = 1.4426950408889634

LANES = 128
SUBLANES = 8
DMA_UNROLL = 8

QUERY_TILE = 128
SB_Q_SPAN = 512
SB_K_SPAN = 512
ML_CHUNK = 256
ROW_TILE_PROJ = 1024
ROW_TILE_MERGE = 256
ROW_TILE_MOE = 256
VMEM_LIMIT = 56 * 1024 * 1024


def _nt(a, b, precision=None):
    return lax.dot_general(a, b, (((1,), (1,)), ((), ())), preferred_element_type=F32, precision=precision)


def _mm(a, b, precision=None):
    return jnp.dot(a, b, preferred_element_type=F32, precision=precision)


def _sigmoid(x):
    return 1.0 / (1.0 + jnp.exp(-x))


def _log_sigmoid(x):
    return jnp.minimum(x, 0.0) - jnp.log(1.0 + jnp.exp(-jnp.abs(x)))


def _ada_kernel(c_ref, w_ref, b_ref, o_ref):
    c = c_ref[...]
    o_ref[...] = _mm(c * _sigmoid(c), w_ref[...], precision=HIGHEST) + b_ref[...]


def _ada(c, ada_w, ada_b):
    bsz = c.shape[0]
    n = ada_w.shape[1]
    return pl.pallas_call(
        _ada_kernel,
        grid=(n // D_MODEL,),
        in_specs=[
            pl.BlockSpec((bsz, D_MODEL), lambda j: (0, 0)),
            pl.BlockSpec((D_MODEL, D_MODEL), lambda j: (0, j)),
            pl.BlockSpec((1, D_MODEL), lambda j: (0, j)),
        ],
        out_specs=pl.BlockSpec((bsz, D_MODEL), lambda j: (0, j)),
        out_shape=jax.ShapeDtypeStruct((bsz, n), F32),
        name="ada",
    )(c, ada_w, ada_b.reshape(1, n))


def _proj_kernel(x_ref, mod_ref, g_ref, w_ref, wg_ref, o_ref, og_ref, h_ref):
    @pl.when(pl.program_id(1) == 0)
    def _():
        x = x_ref[...]
        y = x * lax.rsqrt(jnp.mean(x * x, axis=-1, keepdims=True) + EPS) * g_ref[...]
        m = mod_ref[0]
        hb = (y * (1.0 + m[1:2]) + m[0:1]).astype(BF16)
        h_ref[...] = hb
        og_ref[...] = _mm(hb, wg_ref[...])

    o_ref[...] = _mm(h_ref[...], w_ref[...]).astype(BF16)


def _proj(x2, mod3, norm_g, w_main, w_gate, seq):
    n_tok = x2.shape[0]
    n_main = w_main.shape[1]
    tm = min(ROW_TILE_PROJ, seq)
    tn = D_MODEL
    return pl.pallas_call(
        _proj_kernel,
        grid=(n_tok // tm, n_main // tn),
        in_specs=[
            pl.BlockSpec((tm, D_MODEL), lambda i, j: (i, 0)),
            pl.BlockSpec((1, N_MOD, D_MODEL), lambda i, j: ((i * tm) // seq, 0, 0)),
            pl.BlockSpec((1, D_MODEL), lambda i, j: (0, 0)),
            pl.BlockSpec((D_MODEL, tn), lambda i, j: (0, j)),
            pl.BlockSpec((D_MODEL, LANES), lambda i, j: (0, 0)),
        ],
        out_specs=[
            pl.BlockSpec((tm, tn), lambda i, j: (i, j)),
            pl.BlockSpec((tm, LANES), lambda i, j: (i, 0)),
        ],
        out_shape=[
            jax.ShapeDtypeStruct((n_tok, n_main), BF16),
            jax.ShapeDtypeStruct((n_tok, LANES), F32),
        ],
        scratch_shapes=[pltpu.VMEM((tm, D_MODEL), BF16)],
        compiler_params=pltpu.CompilerParams(
            dimension_semantics=("arbitrary", "arbitrary"), vmem_limit_bytes=VMEM_LIMIT),
        name="proj",
    )(x2, mod3, norm_g.reshape(1, D_MODEL), w_main, w_gate)


def _sb_kernel(q_ref, k_ref, v_ref, uu_ref, o_ref, qh_s, run_s, acc_s):
    seq = q_ref.shape[0]
    qt = QUERY_TILE
    tq, tk = min(SB_Q_SPAN, seq), min(SB_K_SPAN, seq)
    n_heads = LANES // SB_HEAD_DIM
    n_qb, n_kb = tq // qt, tk // qt
    lane = lax.broadcasted_iota(I32, (qt, LANES), 1)
    col = lax.broadcasted_iota(I32, (qt, tk), 1)
    row = lax.broadcasted_iota(I32, (qt, tk), 0)
    uu = uu_ref[...]

    def group(s0, offs):
        k = k_ref[pl.ds(s0, tk), :]
        v = v_ref[pl.ds(s0, tk), :]
        chains = range(n_qb * n_heads)
        masks = [None if offs is None else col < row + offs[qb] for qb in range(n_qb)]
        zs = [_nt(qh_s[ci], k) for ci in chains]
        his, los = [], []
        for ci in chains:
            z = zs[ci]
            nz = -z
            lk = jnp.minimum(nz, 0.0) - jnp.log(1.0 + jnp.exp2(jnp.minimum(z, nz))) * LOG2E
            if masks[ci // n_heads] is not None:
                lk = jnp.where(masks[ci // n_heads], lk, 0.0)
            hi = lk.astype(BF16)
            his.append(hi)
            los.append((lk - hi.astype(F32)).astype(BF16))
        ccs = [[_mm(jnp.concatenate([his[ci][:, b * qt:(b + 1) * qt], los[ci][:, b * qt:(b + 1) * qt]], axis=1), uu)
                for b in range(n_kb)] for ci in chains]
        probs = []
        for ci in chains:
            run = run_s[ci]
            args = [None] * n_kb
            for b in reversed(range(n_kb)):
                args[b] = zs[ci][:, b * qt:(b + 1) * qt] + ccs[ci][b][:, :LANES] + run
                run = run + ccs[ci][b][:, LANES:]
            run_s[ci] = run
            a = jnp.exp2(jnp.concatenate(args, axis=1))
            if masks[ci // n_heads] is not None:
                a = jnp.where(masks[ci // n_heads], a, 0.0)
            probs.append(a.astype(BF16))
        for ci in chains:
            acc_s[ci] = acc_s[ci] + _mm(probs[ci], v)

    def qspan(i, carry):
        r0 = pl.multiple_of(i * tq, tq)
        run_s[...] = jnp.zeros_like(run_s)
        acc_s[...] = jnp.zeros_like(acc_s)
        for qb in range(n_qb):
            qf = q_ref[pl.ds(r0 + qb * qt, qt), :].astype(F32) * (SB_HEAD_DIM ** -0.5 * LOG2E)
            for h in range(n_heads):
                in_head = (lane >= h * SB_HEAD_DIM) & (lane < (h + 1) * SB_HEAD_DIM)
                qh_s[qb * n_heads + h] = jnp.where(in_head, qf, 0.0).astype(BF16)

        n_past = r0 // tk
        d0 = pl.multiple_of(n_past * tk, tk)
        group(d0, [r0 + qb * qt - d0 for qb in range(n_qb)])

        def past(j, c):
            group(pl.multiple_of((n_past - 1 - j) * tk, tk), None)
            return c

        lax.fori_loop(0, n_past, past, 0)
        for qb in range(n_qb):
            o_ref[pl.ds(r0 + qb * qt, qt), :] = jnp.where(
                lane < SB_HEAD_DIM, acc_s[qb * n_heads], acc_s[qb * n_heads + 1]).astype(BF16)
        return carry

    lax.fori_loop(0, seq // tq, qspan, 0)


def _sb_attention(proj, uu, bsz, seq):
    n_pairs = D_MODEL // LANES
    tq, tk = min(SB_Q_SPAN, seq), min(SB_K_SPAN, seq)
    assert tk % tq == 0 and seq % tk == 0 and LANES // SB_HEAD_DIM == 2
    n_chain = (tq // QUERY_TILE) * (LANES // SB_HEAD_DIM)
    return pl.pallas_call(
        _sb_kernel,
        grid=(bsz, n_pairs),
        in_specs=[
            pl.BlockSpec((seq, LANES), lambda b, p: (b, p)),
            pl.BlockSpec((seq, LANES), lambda b, p: (b, n_pairs + p)),
            pl.BlockSpec((seq, LANES), lambda b, p: (b, 2 * n_pairs + p)),
            pl.BlockSpec((2 * LANES, 2 * LANES), lambda b, p: (0, 0)),
        ],
        out_specs=pl.BlockSpec((seq, LANES), lambda b, p: (b, p)),
        out_shape=jax.ShapeDtypeStruct((bsz * seq, D_MODEL), BF16),
        scratch_shapes=[
            pltpu.VMEM((n_chain, QUERY_TILE, LANES), BF16),
            pltpu.VMEM((n_chain, QUERY_TILE, LANES), F32),
            pltpu.VMEM((n_chain, QUERY_TILE, LANES), F32),
        ],
        compiler_params=pltpu.CompilerParams(dimension_semantics=("arbitrary", "arbitrary")),
        name="sb_attn",
    )(proj, proj, proj, uu)


def _mlstm_kernel(q_ref, k_ref, v_ref, og_ref, gt_ref, gb_ref, cwq_ref, cwk_ref, cbq_ref, cbk_ref, ng_ref, tri_ref,
                  y_ref, xq_s, xk_s, c_s, n_s, m_s):
    head = pl.program_id(1)
    seq = q_ref.shape[0]
    cl = min(ML_CHUNK, seq)
    pad = SUBLANES
    xq_s[0:pad, :] = jnp.zeros((pad, ML_HEAD_DIM), F32)
    xk_s[0:pad, :] = jnp.zeros((pad, ML_HEAD_DIM), F32)
    c_s[...] = jnp.zeros_like(c_s)
    n_s[...] = jnp.zeros_like(n_s)
    m_s[...] = jnp.zeros_like(m_s)
    lane = lax.broadcasted_iota(I32, (cl, LANES), 1)
    sel_i = (lane == head).astype(F32)
    sel_f = (lane == head + ML_HEADS).astype(F32)
    tril = lax.broadcasted_iota(I32, (cl, cl), 1) <= lax.broadcasted_iota(I32, (cl, cl), 0)

    def conv_silu(x_s, cw_ref, cb_ref):
        y = cb_ref[...]
        for tap in range(CONV_WIDTH):
            lo = pad - (CONV_WIDTH - 1) + tap
            y = y + x_s[lo:lo + cl, :] * cw_ref[tap:tap + 1, :]
        x_s[0:pad, :] = x_s[cl:cl + pad, :]
        return y * _sigmoid(y)

    def chunk(c, carry):
        r0 = pl.multiple_of(c * cl, cl)
        xq_s[pad:pad + cl, :] = q_ref[pl.ds(r0, cl), :].astype(F32)
        xk_s[pad:pad + cl, :] = k_ref[pl.ds(r0, cl), :].astype(F32)
        qc = conv_silu(xq_s, cwq_ref, cbq_ref)
        kc = conv_silu(xk_s, cwk_ref, cbk_ref) * (ML_HEAD_DIM ** -0.5)
        qb = qc.astype(BF16)
        kb = kc.astype(BF16)
        vb = v_ref[pl.ds(r0, cl), :]

        pre = gt_ref[pl.ds(r0, cl), :] + gb_ref[...]
        bt = _mm(tri_ref[...], _log_sigmoid(pre), precision=HIGHEST)
        b_col = jnp.sum(bt * sel_f, axis=1, keepdims=True)
        i_col = jnp.sum(pre * sel_i, axis=1, keepdims=True)
        g_col = i_col - b_col
        g_row = jnp.transpose(jnp.broadcast_to(g_col, (cl, LANES)))[0:1, :]

        m_prev = m_s[...]
        dmat = jnp.where(tril, b_col + g_row, -jnp.inf)
        m_row = jnp.maximum(b_col + m_prev, jnp.max(dmat, axis=1, keepdims=True))
        w_inter = jnp.exp(b_col + m_prev - m_row)
        s_mat = _nt(qb, kb) * jnp.exp(dmat - m_row)
        num = w_inter * _mm(qb, c_s[...].astype(BF16)) + _mm(s_mat.astype(BF16), vb)
        den = w_inter * jnp.sum(qc * n_s[...], axis=1, keepdims=True) + jnp.sum(s_mat, axis=1, keepdims=True)
        hh = num * (1.0 / jnp.maximum(jnp.abs(den), jnp.exp(-m_row)))

        m_new = m_row[cl - 1:cl, :]
        b_last = b_col[cl - 1:cl, :]
        decay = jnp.exp(b_last + m_prev - m_new)
        wk = jnp.exp(b_last + g_col - m_new) * kc
        c_s[...] = decay * c_s[...] + lax.dot_general(
            wk.astype(BF16), vb, (((0,), (0,)), ((), ())), preferred_element_type=F32)
        n_s[...] = decay * n_s[...] + jnp.sum(wk, axis=0, keepdims=True)
        m_s[...] = m_new

        hn = hh * lax.rsqrt(jnp.mean(hh * hh, axis=1, keepdims=True) + EPS) * ng_ref[...]
        y_ref[pl.ds(r0, cl), :] = (hn * _sigmoid(og_ref[pl.ds(r0, cl), :].astype(F32))).astype(BF16)
        return carry

    lax.fori_loop(0, seq // cl, chunk, 0)


def _mlstm(proj, gates, gate_bias, conv_w, conv_b, norm_g, tri, bsz, seq):
    hd = ML_HEAD_DIM
    cl = min(ML_CHUNK, seq)
    col0 = 3 * D_MODEL // hd
    row_spec = lambda off: pl.BlockSpec((seq, hd), lambda b, h: (b, off + h))
    return pl.pallas_call(
        _mlstm_kernel,
        grid=(bsz, ML_HEADS),
        in_specs=[
            row_spec(col0), row_spec(col0 + ML_HEADS), row_spec(col0 + 2 * ML_HEADS), row_spec(col0 + 3 * ML_HEADS),
            pl.BlockSpec((seq, LANES), lambda b, h: (b, 0)),
            pl.BlockSpec((1, LANES), lambda b, h: (0, 0)),
            pl.BlockSpec((CONV_WIDTH, hd), lambda b, h: (0, h)),
            pl.BlockSpec((CONV_WIDTH, hd), lambda b, h: (0, ML_HEADS + h)),
            pl.BlockSpec((1, hd), lambda b, h: (0, h)),
            pl.BlockSpec((1, hd), lambda b, h: (0, ML_HEADS + h)),
            pl.BlockSpec((1, hd), lambda b, h: (0, h)),
            pl.BlockSpec((cl, cl), lambda b, h: (0, 0)),
        ],
        out_specs=pl.BlockSpec((seq, hd), lambda b, h: (b, h)),
        out_shape=jax.ShapeDtypeStruct((bsz * seq, D_MODEL), BF16),
        scratch_shapes=[
            pltpu.VMEM((cl + 2 * SUBLANES, hd), F32),
            pltpu.VMEM((cl + 2 * SUBLANES, hd), F32),
            pltpu.VMEM((hd, hd), F32),
            pltpu.VMEM((1, hd), F32),
            pltpu.VMEM((1, 1), F32),
        ],
        compiler_params=pltpu.CompilerParams(
            dimension_semantics=("arbitrary", "arbitrary"), vmem_limit_bytes=VMEM_LIMIT),
        name="mlstm",
    )(proj, proj, proj, proj, gates, gate_bias, conv_w, conv_w,
      conv_b.reshape(1, -1), conv_b.reshape(1, -1), norm_g.reshape(1, -1), tri)


def _merge_kernel(ya_ref, yb_ref, ga_ref, gb_ref, x_ref, mod_ref, wa_ref, wb_ref, wo_ref, n2_ref, rwt_ref, rb_ref,
                  su_ref, x1_ref, h2_ref, idx_ref, wt_ref, rank_ref, cnt_ref, carry_s):
    @pl.when(pl.program_id(0) == 0)
    def _():
        carry_s[...] = jnp.zeros_like(carry_s)

    tm = x_ref.shape[0]
    a = _mm(ya_ref[...], wa_ref[...])
    b = _mm(yb_ref[...], wb_ref[...])
    merged = _sigmoid(ga_ref[...].astype(F32)) * a + _sigmoid(gb_ref[...].astype(F32)) * b
    m = mod_ref[0]
    x1 = x_ref[...] + m[2:3] * _mm(merged.astype(BF16), wo_ref[...])
    x1_ref[...] = x1
    y = x1 * lax.rsqrt(jnp.mean(x1 * x1, axis=-1, keepdims=True) + EPS) * n2_ref[...]
    h2 = y * (1.0 + m[4:5]) + m[3:4]
    h2_ref[...] = h2

    logits = _nt(rwt_ref[...], h2, precision=HIGHEST) + rb_ref[...]
    ie = lax.broadcasted_iota(I32, (N_EXPERTS, tm), 0).astype(F32)
    idxs, vals = [], []
    for _ in range(TOP_K):
        mx = jnp.max(logits, axis=0, keepdims=True)
        am = jnp.min(jnp.where(logits == mx, ie, float(N_EXPERTS)), axis=0, keepdims=True)
        idxs.append(am)
        vals.append(mx)
        logits = jnp.where(ie == am, -jnp.inf, logits)
    exps = [jnp.exp(v - vals[0]) for v in vals]
    inv = 1.0 / (exps[0] + exps[1] + exps[2] + exps[3])
    wt_ref[...] = jnp.concatenate([e * inv for e in exps], axis=0)
    idx_ref[...] = jnp.concatenate(idxs, axis=0).astype(I32)

    onehot = jnp.zeros((N_EXPERTS, tm), F32)
    for am in idxs:
        onehot = onehot + (ie == am).astype(F32)
    before = _mm(onehot.astype(BF16), su_ref[...]) + carry_s[:, 0:1]
    ranks = [jnp.sum(jnp.where(ie == am, before, 0.0), axis=0, keepdims=True) for am in idxs]
    rank_ref[...] = jnp.concatenate(ranks, axis=0).astype(I32)
    carry_s[...] = carry_s[...] + jnp.sum(onehot, axis=1, keepdims=True)
    cnt_ref[...] = carry_s[...]


def _merge(ya, yb, proj, x2, mod3, w_a, w_b, w_o, norm2_g, router_wt, router_b, su, seq):
    n_tok = x2.shape[0]
    tm = min(ROW_TILE_MERGE, seq)
    d = D_MODEL
    gcol = 7 * d // d
    row = lambda j: pl.BlockSpec((tm, d), lambda i: (i, j))
    const = lambda shape: pl.BlockSpec(shape, lambda i: tuple(0 for _ in shape))
    tok4 = pl.BlockSpec((TOP_K, tm), lambda i: (0, i))
    return pl.pallas_call(
        _merge_kernel,
        grid=(n_tok // tm,),
        in_specs=[
            row(0), row(0), row(gcol), row(gcol + 1), row(0),
            pl.BlockSpec((1, N_MOD, d), lambda i: ((i * tm) // seq, 0, 0)),
            const((d, d)), const((d, d)), const((d, d)), const((1, d)),
            const((N_EXPERTS, d)), const((N_EXPERTS, 1)), const((tm, tm)),
        ],
        out_specs=[
            row(0), row(0),
            tok4, tok4, tok4,
            const((N_EXPERTS, LANES)),
        ],
        out_shape=[
            jax.ShapeDtypeStruct((n_tok, d), F32),
            jax.ShapeDtypeStruct((n_tok, d), F32),
            jax.ShapeDtypeStruct((TOP_K, n_tok), I32),
            jax.ShapeDtypeStruct((TOP_K, n_tok), F32),
            jax.ShapeDtypeStruct((TOP_K, n_tok), I32),
            jax.ShapeDtypeStruct((N_EXPERTS, LANES), F32),
        ],
        scratch_shapes=[pltpu.VMEM((N_EXPERTS, LANES), F32)],
        compiler_params=pltpu.CompilerParams(dimension_semantics=("arbitrary",), vmem_limit_bytes=VMEM_LIMIT),
        name="merge",
    )(ya, yb, proj, proj, x2, mod3, w_a, w_b, w_o, norm2_g.reshape(1, d), router_wt, router_b.reshape(-1, 1), su)


def _scatter_kernel(start_ref, fill_lo_ref, fill_hi_ref, nb_ref, idx_ref, rank_ref, h2_ref, xs_ref, zero_s, sem, zsem):
    tm = h2_ref.shape[0]
    bm = zero_s.shape[0]

    @pl.when(pl.program_id(0) == 0)
    def _():
        zero_s[...] = jnp.zeros_like(zero_s)

        def per_expert(e, total):
            lo, hi = fill_lo_ref[e], fill_hi_ref[e]

            def fill(r, c):
                pltpu.make_async_copy(zero_s.at[pl.ds(0, 1), :], xs_ref.at[pl.ds(r, 1), :], sem).start()
                return c

            lax.fori_loop(lo, hi, fill, 0)
            return total + (hi - lo)

        total = lax.fori_loop(0, N_EXPERTS, per_expert, 0)

        def drain(r, c):
            pltpu.make_async_copy(zero_s.at[pl.ds(0, 1), :], xs_ref.at[pl.ds(0, 1), :], sem).wait()
            return c

        lax.fori_loop(0, total, drain, 0)

        n_blocks = xs_ref.shape[0] // bm

        def fill_block(j, c):
            pltpu.make_async_copy(zero_s, xs_ref.at[pl.ds(pl.multiple_of(j * bm, bm), bm), :], zsem).start()
            return c

        lax.fori_loop(nb_ref[0], n_blocks, fill_block, 0)

        def drain_block(j, c):
            pltpu.make_async_copy(zero_s, xs_ref.at[pl.ds(0, bm), :], zsem).wait()
            return c

        lax.fori_loop(nb_ref[0], n_blocks, drain_block, 0)

    def send(g, c):
        for u in range(DMA_UNROLL):
            t = g * DMA_UNROLL + u
            for r in range(TOP_K):
                dst = start_ref[idx_ref[r, t]] + rank_ref[r, t]
                pltpu.make_async_copy(h2_ref.at[pl.ds(t, 1), :], xs_ref.at[pl.ds(dst, 1), :], sem).start()
        return c

    lax.fori_loop(0, tm // DMA_UNROLL, send, 0)

    def drain(g, c):
        for _ in range(DMA_UNROLL * TOP_K):
            pltpu.make_async_copy(h2_ref.at[pl.ds(0, 1), :], xs_ref.at[pl.ds(0, 1), :], sem).wait()
        return c

    lax.fori_loop(0, tm // DMA_UNROLL, drain, 0)


def _scatter(start, fill_lo, fill_hi, n_used, idx, rank, h2s, n_rows, seq):
    n_tok = h2s.shape[0]
    tm = min(ROW_TILE_MOE, seq)
    smem = lambda: pl.BlockSpec(memory_space=pltpu.SMEM)
    tok4 = pl.BlockSpec((TOP_K, tm), lambda i: (0, i), memory_space=pltpu.SMEM)
    return pl.pallas_call(
        _scatter_kernel,
        grid=(n_tok // tm,),
        in_specs=[smem(), smem(), smem(), smem(), tok4, tok4, pl.BlockSpec((tm, D_MODEL), lambda i: (i, 0))],
        out_specs=pl.BlockSpec(memory_space=pl.ANY),
        out_shape=jax.ShapeDtypeStruct((n_rows, D_MODEL), F32),
        scratch_shapes=[pltpu.VMEM((ROW_TILE_MOE, D_MODEL), F32), pltpu.SemaphoreType.DMA,
                        pltpu.SemaphoreType.DMA],
        compiler_params=pltpu.CompilerParams(dimension_semantics=("arbitrary",)),
        name="scatter",
    )(start, fill_lo, fill_hi, n_used, idx, rank, h2s)


def _expert_kernel(be_ref, nb_ref, xs_ref, w1_ref, b1_ref, w2_ref, b2_ref, ys_ref, w1_s, w2_s):
    j = pl.program_id(0)
    changed = jnp.logical_or(j == 0, be_ref[j] != be_ref[jnp.maximum(j - 1, 0)])

    @pl.when(changed)
    def _():
        w1_s[...] = w1_ref[...].astype(BF16)
        w2_s[...] = w2_ref[...].astype(BF16)

    @pl.when(j < nb_ref[0])
    def _():
        a = _mm(xs_ref[...].astype(BF16), w1_s[...]) + b1_ref[...]
        glu = jnp.minimum(a[:, :D_FF], SWIGLU_LIMIT)
        lin = jnp.clip(a[:, D_FF:], -SWIGLU_LIMIT, SWIGLU_LIMIT)
        hmid = (lin + 1.0) * (glu * _sigmoid(SWIGLU_ALPHA * glu))
        ys_ref[...] = _mm(hmid.astype(BF16), w2_s[...]) + b2_ref[...]

    @pl.when(j >= nb_ref[0])
    def _():
        ys_ref[...] = jnp.zeros_like(ys_ref)


def _experts(block_e, n_blocks_used, xs, w1, b1, w2, b2):
    n_rows = xs.shape[0]
    bm = ROW_TILE_MOE
    d = D_MODEL
    blk = lambda j, be, nb: (j, 0)
    exp = lambda j, be, nb: (be[j], 0, 0)
    return pl.pallas_call(
        _expert_kernel,
        grid_spec=pltpu.PrefetchScalarGridSpec(
            num_scalar_prefetch=2,
            grid=(n_rows // bm,),
            in_specs=[
                pl.BlockSpec((bm, d), blk),
                pl.BlockSpec((None, d, 2 * D_FF), exp),
                pl.BlockSpec((None, 1, 2 * D_FF), exp),
                pl.BlockSpec((None, D_FF, d), exp),
                pl.BlockSpec((None, 1, d), exp),
            ],
            out_specs=pl.BlockSpec((bm, d), blk),
            scratch_shapes=[pltpu.VMEM((d, 2 * D_FF), BF16), pltpu.VMEM((D_FF, d), BF16)],
        ),
        out_shape=jax.ShapeDtypeStruct((n_rows, d), F32),
        compiler_params=pltpu.CompilerParams(dimension_semantics=("arbitrary",), vmem_limit_bytes=VMEM_LIMIT),
        name="experts",
    )(block_e, n_blocks_used, xs, w1, b1.reshape(N_EXPERTS, 1, -1), w2, b2.reshape(N_EXPERTS, 1, -1))


def _combine_kernel(start_ref, idx_ref, rank_ref, wt_ref, ys_ref, x1_ref, mod_ref, fg_ref, o_ref, buf_s, sem):
    tm = x1_ref.shape[0]

    def fetch(g, c):
        for u in range(DMA_UNROLL):
            t = g * DMA_UNROLL + u
            for r in range(TOP_K):
                src = start_ref[idx_ref[r, t]] + rank_ref[r, t]
                pltpu.make_async_copy(ys_ref.at[pl.ds(src, 1), :], buf_s.at[r, pl.ds(t, 1), :], sem).start()
        return c

    lax.fori_loop(0, tm // DMA_UNROLL, fetch, 0)

    def drain(g, c):
        for _ in range(DMA_UNROLL * TOP_K):
            pltpu.make_async_copy(ys_ref.at[pl.ds(0, 1), :], buf_s.at[0, pl.ds(0, 1), :], sem).wait()
        return c

    lax.fori_loop(0, tm // DMA_UNROLL, drain, 0)

    wt = wt_ref[...]
    moe = wt[:, 0:1] * buf_s[0]
    for r in range(1, TOP_K):
        moe = moe + wt[:, r:r + 1] * buf_s[r]
    x2 = x1_ref[...] + mod_ref[0][5:6] * moe
    o_ref[...] = x2 * lax.rsqrt(jnp.mean(x2 * x2, axis=-1, keepdims=True) + EPS) * fg_ref[...]


def _combine(start, idx, rank, wt, ys, x1, mod3, final_g, seq):
    n_tok = x1.shape[0]
    tm = min(ROW_TILE_MOE, seq)
    d = D_MODEL
    tok4 = pl.BlockSpec((TOP_K, tm), lambda i: (0, i), memory_space=pltpu.SMEM)
    return pl.pallas_call(
        _combine_kernel,
        grid=(n_tok // tm,),
        in_specs=[
            pl.BlockSpec(memory_space=pltpu.SMEM), tok4, tok4,
            pl.BlockSpec((tm, TOP_K), lambda i: (i, 0)),
            pl.BlockSpec(memory_space=pl.ANY),
            pl.BlockSpec((tm, d), lambda i: (i, 0)),
            pl.BlockSpec((1, N_MOD, d), lambda i: ((i * tm) // seq, 0, 0)),
            pl.BlockSpec((1, d), lambda i: (0, 0)),
        ],
        out_specs=pl.BlockSpec((tm, d), lambda i: (i, 0)),
        out_shape=jax.ShapeDtypeStruct((n_tok, d), F32),
        scratch_shapes=[pltpu.VMEM((TOP_K, tm, d), F32), pltpu.SemaphoreType.DMA],
        compiler_params=pltpu.CompilerParams(dimension_semantics=("arbitrary",), vmem_limit_bytes=VMEM_LIMIT),
        name="combine",
    )(start, idx, rank, wt.T, ys, x1, mod3, final_g.reshape(1, d))


def _layer(x2, mod3, bsz, seq, norm1_g, w_in, conv_w, conv_b, ml_b_i, ml_b_f, ml_norm_g, w_branch_a, w_branch_b,
           w_out, norm2_g, router_w, router_b, expert_w1, expert_b1, expert_w2, expert_b2):
    d = D_MODEL
    n_tok = bsz * seq
    gate0 = 7 * d
    n_gate = 2 * ML_HEADS
    w_main = jnp.concatenate([w_in[:, :gate0], w_in[:, gate0 + n_gate:]], axis=1).astype(BF16)
    w_gate = jnp.pad(w_in[:, gate0:gate0 + n_gate], ((0, 0), (0, LANES - n_gate))).astype(BF16)
    gate_bias = jnp.pad(jnp.concatenate([ml_b_i, ml_b_f]), (0, LANES - n_gate)).reshape(1, LANES)

    proj, gates = _proj(x2, mod3, norm1_g, w_main, w_gate, seq)

    ones = jnp.ones((LANES, LANES), F32)
    u_half = jnp.concatenate([jnp.tril(ones), ones], axis=1)
    uu = jnp.concatenate([u_half, u_half], axis=0).astype(BF16)
    ya = _sb_attention(proj, uu, bsz, seq)

    cl = min(ML_CHUNK, seq)
    tri = jnp.tril(jnp.ones((cl, cl), F32))
    yb = _mlstm(proj, gates, gate_bias, conv_w, conv_b, ml_norm_g, tri, bsz, seq)

    tm = min(ROW_TILE_MERGE, seq)
    su = jnp.triu(jnp.ones((tm, tm), F32), k=1).astype(BF16)
    x1, h2s, idx, wt, rank, cnt = _merge(
        ya, yb, proj, x2, mod3, w_branch_a.astype(BF16), w_branch_b.astype(BF16), w_out.astype(BF16),
        norm2_g, router_w.T, router_b, su, seq)

    bm = ROW_TILE_MOE
    counts = cnt[:, 0].astype(I32)
    padded = (counts + bm - 1) // bm * bm
    pad_end = jnp.cumsum(padded)
    start = pad_end - padded
    n_rows = (n_tok * TOP_K + N_EXPERTS * (bm - 1)) // bm * bm
    n_blocks = n_rows // bm
    n_used = jnp.maximum(pad_end[-1] // bm, 1).astype(I32)
    blk = jnp.minimum(jnp.arange(n_blocks, dtype=I32), n_used - 1)
    block_e = jnp.minimum(jnp.sum((pad_end[None, :] <= (blk * bm)[:, None]).astype(I32), axis=1), N_EXPERTS - 1)

    xs = _scatter(start, start + counts, pad_end, n_used.reshape(1), idx, rank, h2s, n_rows, seq)
    ys = _experts(block_e, n_used.reshape(1), xs, expert_w1, expert_b1, expert_w2, expert_b2)
    return start, idx, rank, wt, ys, x1


def kernel(x, c, ada_w, ada_b, norm1_g, w_in, conv_w, conv_b, ml_b_i, ml_b_f, ml_norm_g, w_branch_a, w_branch_b, w_out, norm2_g, router_w, router_b, expert_w1, expert_b1, expert_w2, expert_b2, final_g):
    bsz, seq, d = x.shape
    depth = ada_w.shape[0]
    assert d == D_MODEL and depth == 1 and seq % QUERY_TILE == 0
    x2 = x.reshape(bsz * seq, d)
    mod3 = _ada(c, ada_w[0], ada_b[0]).reshape(bsz, N_MOD, d)
    start, idx, rank, wt, ys, x1 = _layer(
        x2, mod3, bsz, seq, norm1_g[0], w_in[0], conv_w[0], conv_b[0], ml_b_i[0], ml_b_f[0], ml_norm_g[0],
        w_branch_a[0], w_branch_b[0], w_out[0], norm2_g[0], router_w[0], router_b[0],
        expert_w1[0], expert_b1[0], expert_w2[0], expert_b2[0])
    out = _combine(start, idx, rank, wt, ys, x1, mod3, final_g, seq)
    return out.reshape(bsz, seq, d)
```

```python
import functools

import jax
import jax.numpy as jnp
from jax import lax
from jax.experimental import pallas as pl
from jax.experimental.pallas import tpu as pltpu

F32 = jnp.float32
BF16 = jnp.bfloat16
I32 = jnp.int32
HIGHEST = lax.Precision.HIGHEST

D_MODEL = 1024
SB_HEAD_DIM = 64
ML_HEADS = 4
ML_HEAD_DIM = 256
CONV_WIDTH = 4
N_EXPERTS = 32
TOP_K = 4
D_FF = 1024
SWIGLU_LIMIT = 7.0
SWIGLU_ALPHA = 1.702
N_MOD = 6
EPS = 1e-6
LOG2E = 1.4426950408889634

LANES = 128
SUBLANES = 8
DMA_UNROLL = 8

QUERY_TILE = 128
SB_Q_SPAN = 512
SB_K_SPAN = 512
ML_CHUNK = 256
ROW_TILE_PROJ = 1024
ROW_TILE_MERGE = 512
ROW_TILE_MOE = 256
VMEM_LIMIT = 56 * 1024 * 1024


def _nt(a, b, precision=None):
    return lax.dot_general(a, b, (((1,), (1,)), ((), ())), preferred_element_type=F32, precision=precision)


def _mm(a, b, precision=None):
    return jnp.dot(a, b, preferred_element_type=F32, precision=precision)


def _sigmoid(x):
    return 1.0 / (1.0 + jnp.exp(-x))


def _log_sigmoid(x):
    return jnp.minimum(x, 0.0) - jnp.log(1.0 + jnp.exp(-jnp.abs(x)))


def _ada_kernel(c_ref, w_ref, b_ref, o_ref):
    c = c_ref[...]
    o_ref[...] = _mm(c * _sigmoid(c), w_ref[...], precision=HIGHEST) + b_ref[...]


def _ada(c, ada_w, ada_b):
    bsz = c.shape[0]
    n = ada_w.shape[1]
    return pl.pallas_call(
        _ada_kernel,
        grid=(n // D_MODEL,),
        in_specs=[
            pl.BlockSpec((bsz, D_MODEL), lambda j: (0, 0)),
            pl.BlockSpec((D_MODEL, D_MODEL), lambda j: (0, j)),
            pl.BlockSpec((1, D_MODEL), lambda j: (0, j)),
        ],
        out_specs=pl.BlockSpec((bsz, D_MODEL), lambda j: (0, j)),
        out_shape=jax.ShapeDtypeStruct((bsz, n), F32),
        name="ada",
    )(c, ada_w, ada_b.reshape(1, n))


def _proj_kernel(x_ref, mod_ref, g_ref, w_ref, wg_ref, o_ref, og_ref, h_ref):
    @pl.when(pl.program_id(1) == 0)
    def _():
        x = x_ref[...]
        y = x * lax.rsqrt(jnp.mean(x * x, axis=-1, keepdims=True) + EPS) * g_ref[...]
        m = mod_ref[0]
        hb = (y * (1.0 + m[1:2]) + m[0:1]).astype(BF16)
        h_ref[...] = hb
        og_ref[...] = _mm(hb, wg_ref[...])

    o_ref[...] = _mm(h_ref[...], w_ref[...]).astype(BF16)


def _proj(x2, mod3, norm_g, w_main, w_gate, seq):
    n_tok = x2.shape[0]
    n_main = w_main.shape[1]
    tm = min(ROW_TILE_PROJ, seq)
    tn = D_MODEL
    return pl.pallas_call(
        _proj_kernel,
        grid=(n_tok // tm, n_main // tn),
        in_specs=[
            pl.BlockSpec((tm, D_MODEL), lambda i, j: (i, 0)),
            pl.BlockSpec((1, N_MOD, D_MODEL), lambda i, j: ((i * tm) // seq, 0, 0)),
            pl.BlockSpec((1, D_MODEL), lambda i, j: (0, 0)),
            pl.BlockSpec((D_MODEL, tn), lambda i, j: (0, j)),
            pl.BlockSpec((D_MODEL, LANES), lambda i, j: (0, 0)),
        ],
        out_specs=[
            pl.BlockSpec((tm, tn), lambda i, j: (i, j)),
            pl.BlockSpec((tm, LANES), lambda i, j: (i, 0)),
        ],
        out_shape=[
            jax.ShapeDtypeStruct((n_tok, n_main), BF16),
            jax.ShapeDtypeStruct((n_tok, LANES), F32),
        ],
        scratch_shapes=[pltpu.VMEM((tm, D_MODEL), BF16)],
        compiler_params=pltpu.CompilerParams(
            dimension_semantics=("arbitrary", "arbitrary"), vmem_limit_bytes=VMEM_LIMIT),
        name="proj",
    )(x2, mod3, norm_g.reshape(1, D_MODEL), w_main, w_gate)


def _sb_kernel(q_ref, k_ref, v_ref, uu_ref, o_ref, qh_s, run_s, acc_s):
    seq = q_ref.shape[0]
    qt = QUERY_TILE
    tq, tk = min(SB_Q_SPAN, seq), min(SB_K_SPAN, seq)
    n_heads = LANES // SB_HEAD_DIM
    n_qb, n_kb = tq // qt, tk // qt
    lane = lax.broadcasted_iota(I32, (qt, LANES), 1)
    strict = lane < lax.broadcasted_iota(I32, (qt, LANES), 0)
    uu = uu_ref[...]

    def group(s0, diag):
        k = k_ref[pl.ds(s0, tk), :]
        v = v_ref[pl.ds(s0, tk), :]
        chains = range(n_qb * n_heads)
        nbs = [ci // n_heads + 1 if diag else n_kb for ci in chains]
        blk = lambda x, b: x[:, b * qt:(b + 1) * qt]
        zs = [_nt(qh_s[ci], k[:nbs[ci] * qt]) for ci in chains]
        his, los = [], []
        for ci in chains:
            z = zs[ci]
            nz = -z
            lk = jnp.minimum(nz, 0.0) - jnp.log(1.0 + jnp.exp2(jnp.minimum(z, nz))) * LOG2E
            lks = [blk(lk, b) for b in range(nbs[ci])]
            if diag:
                lks[-1] = jnp.where(strict, lks[-1], 0.0)
            his.append([x.astype(BF16) for x in lks])
            los.append([(x - h.astype(F32)).astype(BF16) for x, h in zip(lks, his[-1])])
        ccs = [[_mm(jnp.concatenate([his[ci][b], los[ci][b]], axis=1), uu) for b in range(nbs[ci])]
               for ci in chains]
        probs = []
        for ci in chains:
            run = run_s[ci]
            a = [None] * nbs[ci]
            for b in reversed(range(nbs[ci])):
                a[b] = jnp.exp2(blk(zs[ci], b) + ccs[ci][b][:, :LANES] + run)
                run = run + ccs[ci][b][:, LANES:]
            run_s[ci] = run
            if diag:
                a[-1] = jnp.where(strict, a[-1], 0.0)
            probs.append(jnp.concatenate(a, axis=1).astype(BF16))
        for ci in chains:
            acc_s[ci] = acc_s[ci] + _mm(probs[ci], v[:nbs[ci] * qt])

    def qspan(i, carry):
        r0 = pl.multiple_of(i * tq, tq)
        run_s[...] = jnp.zeros_like(run_s)
        acc_s[...] = jnp.zeros_like(acc_s)
        for qb in range(n_qb):
            qf = q_ref[pl.ds(r0 + qb * qt, qt), :].astype(F32) * (SB_HEAD_DIM ** -0.5 * LOG2E)
            for h in range(n_heads):
                in_head = (lane >= h * SB_HEAD_DIM) & (lane < (h + 1) * SB_HEAD_DIM)
                qh_s[qb * n_heads + h] = jnp.where(in_head, qf, 0.0).astype(BF16)

        group(r0, True)

        def past(j, c):
            group(pl.multiple_of((i - 1 - j) * tk, tk), False)
            return c

        lax.fori_loop(0, i, past, 0)
        for qb in range(n_qb):
            o_ref[pl.ds(r0 + qb * qt, qt), :] = jnp.where(
                lane < SB_HEAD_DIM, acc_s[qb * n_heads], acc_s[qb * n_heads + 1]).astype(BF16)
        return carry

    lax.fori_loop(0, seq // tq, qspan, 0)


def _sb_attention(proj, uu, bsz, seq):
    n_pairs = D_MODEL // LANES
    tq, tk = min(SB_Q_SPAN, seq), min(SB_K_SPAN, seq)
    assert tk == tq and seq % tk == 0 and LANES // SB_HEAD_DIM == 2
    n_chain = (tq // QUERY_TILE) * (LANES // SB_HEAD_DIM)
    return pl.pallas_call(
        _sb_kernel,
        grid=(bsz, n_pairs),
        in_specs=[
            pl.BlockSpec((seq, LANES), lambda b, p: (b, p)),
            pl.BlockSpec((seq, LANES), lambda b, p: (b, n_pairs + p)),
            pl.BlockSpec((seq, LANES), lambda b, p: (b, 2 * n_pairs + p)),
            pl.BlockSpec((2 * LANES, 2 * LANES), lambda b, p: (0, 0)),
        ],
        out_specs=pl.BlockSpec((seq, LANES), lambda b, p: (b, p)),
        out_shape=jax.ShapeDtypeStruct((bsz * seq, D_MODEL), BF16),
        scratch_shapes=[
            pltpu.VMEM((n_chain, QUERY_TILE, LANES), BF16),
            pltpu.VMEM((n_chain, QUERY_TILE, LANES), F32),
            pltpu.VMEM((n_chain, QUERY_TILE, LANES), F32),
        ],
        compiler_params=pltpu.CompilerParams(dimension_semantics=("arbitrary", "arbitrary")),
        name="sb_attn",
    )(proj, proj, proj, uu)


def _mlstm_kernel(q_ref, k_ref, v_ref, og_ref, gt_ref, gb_ref, cwq_ref, cwk_ref, cbq_ref, cbk_ref, ng_ref, tri_ref,
                  y_ref, xq_s, xk_s, c_s, n_s, m_s):
    head = pl.program_id(1)
    seq = q_ref.shape[0]
    cl = min(ML_CHUNK, seq)
    pad = SUBLANES
    xq_s[0:pad, :] = jnp.zeros((pad, ML_HEAD_DIM), F32)
    xk_s[0:pad, :] = jnp.zeros((pad, ML_HEAD_DIM), F32)
    c_s[...] = jnp.zeros_like(c_s)
    n_s[...] = jnp.zeros_like(n_s)
    m_s[...] = jnp.zeros_like(m_s)
    lane = lax.broadcasted_iota(I32, (cl, LANES), 1)
    sel_i = (lane == head).astype(F32)
    sel_f = (lane == head + ML_HEADS).astype(F32)
    tril = lax.broadcasted_iota(I32, (cl, cl), 1) <= lax.broadcasted_iota(I32, (cl, cl), 0)

    def conv_silu(x_s, cw_ref, cb_ref):
        y = cb_ref[...]
        for tap in range(CONV_WIDTH):
            lo = pad - (CONV_WIDTH - 1) + tap
            y = y + x_s[lo:lo + cl, :] * cw_ref[tap:tap + 1, :]
        x_s[0:pad, :] = x_s[cl:cl + pad, :]
        return y * _sigmoid(y)

    def chunk(c, carry):
        r0 = pl.multiple_of(c * cl, cl)
        xq_s[pad:pad + cl, :] = q_ref[pl.ds(r0, cl), :].astype(F32)
        xk_s[pad:pad + cl, :] = k_ref[pl.ds(r0, cl), :].astype(F32)
        qc = conv_silu(xq_s, cwq_ref, cbq_ref)
        kc = conv_silu(xk_s, cwk_ref, cbk_ref) * (ML_HEAD_DIM ** -0.5)
        qb = qc.astype(BF16)
        kb = kc.astype(BF16)
        vb = v_ref[pl.ds(r0, cl), :]

        pre = gt_ref[pl.ds(r0, cl), :] + gb_ref[...]
        bt = _mm(tri_ref[...], _log_sigmoid(pre), precision=HIGHEST)
        b_col = jnp.sum(bt * sel_f, axis=1, keepdims=True)
        i_col = jnp.sum(pre * sel_i, axis=1, keepdims=True)
        g_col = i_col - b_col
        g_row = jnp.transpose(jnp.broadcast_to(g_col, (cl, LANES)))[0:1, :]

        m_prev = m_s[...]
        dmat = jnp.where(tril, b_col + g_row, -jnp.inf)
        m_row = jnp.maximum(b_col + m_prev, jnp.max(dmat, axis=1, keepdims=True))
        w_inter = jnp.exp(b_col + m_prev - m_row)
        s_mat = _nt(qb, kb) * jnp.exp(dmat - m_row)
        num = w_inter * _mm(qb, c_s[...].astype(BF16)) + _mm(s_mat.astype(BF16), vb)
        den = w_inter * jnp.sum(qc * n_s[...], axis=1, keepdims=True) + jnp.sum(s_mat, axis=1, keepdims=True)
        hh = num * (1.0 / jnp.maximum(jnp.abs(den), jnp.exp(-m_row)))

        m_new = m_row[cl - 1:cl, :]
        b_last = b_col[cl - 1:cl, :]
        decay = jnp.exp(b_last + m_prev - m_new)
        wk = jnp.exp(b_last + g_col - m_new) * kc
        c_s[...] = decay * c_s[...] + lax.dot_general(
            wk.astype(BF16), vb, (((0,), (0,)), ((), ())), preferred_element_type=F32)
        n_s[...] = decay * n_s[...] + jnp.sum(wk, axis=0, keepdims=True)
        m_s[...] = m_new

        hn = hh * lax.rsqrt(jnp.mean(hh * hh, axis=1, keepdims=True) + EPS) * ng_ref[...]
        y_ref[pl.ds(r0, cl), :] = (hn * _sigmoid(og_ref[pl.ds(r0, cl), :].astype(F32))).astype(BF16)
        return carry

    lax.fori_loop(0, seq // cl, chunk, 0)


def _mlstm(proj, gates, gate_bias, conv_w, conv_b, norm_g, tri, bsz, seq):
    hd = ML_HEAD_DIM
    cl = min(ML_CHUNK, seq)
    col0 = 3 * D_MODEL // hd
    row_spec = lambda off: pl.BlockSpec((seq, hd), lambda b, h: (b, off + h))
    return pl.pallas_call(
        _mlstm_kernel,
        grid=(bsz, ML_HEADS),
        in_specs=[
            row_spec(col0), row_spec(col0 + ML_HEADS), row_spec(col0 + 2 * ML_HEADS), row_spec(col0 + 3 * ML_HEADS),
            pl.BlockSpec((seq, LANES), lambda b, h: (b, 0)),
            pl.BlockSpec((1, LANES), lambda b, h: (0, 0)),
            pl.BlockSpec((CONV_WIDTH, hd), lambda b, h: (0, h)),
            pl.BlockSpec((CONV_WIDTH, hd), lambda b, h: (0, ML_HEADS + h)),
            pl.BlockSpec((1, hd), lambda b, h: (0, h)),
            pl.BlockSpec((1, hd), lambda b, h: (0, ML_HEADS + h)),
            pl.BlockSpec((1, hd), lambda b, h: (0, h)),
            pl.BlockSpec((cl, cl), lambda b, h: (0, 0)),
        ],
        out_specs=pl.BlockSpec((seq, hd), lambda b, h: (b, h)),
        out_shape=jax.ShapeDtypeStruct((bsz * seq, D_MODEL), BF16),
        scratch_shapes=[
            pltpu.VMEM((cl + 2 * SUBLANES, hd), F32),
            pltpu.VMEM((cl + 2 * SUBLANES, hd), F32),
            pltpu.VMEM((hd, hd), F32),
            pltpu.VMEM((1, hd), F32),
            pltpu.VMEM((1, 1), F32),
        ],
        compiler_params=pltpu.CompilerParams(
            dimension_semantics=("arbitrary", "arbitrary"), vmem_limit_bytes=VMEM_LIMIT),
        name="mlstm",
    )(proj, proj, proj, proj, gates, gate_bias, conv_w, conv_w,
      conv_b.reshape(1, -1), conv_b.reshape(1, -1), norm_g.reshape(1, -1), tri)


def _merge_kernel(ya_ref, yb_ref, ga_ref, gb_ref, x_ref, mod_ref, wa_ref, wb_ref, wo_ref, n2_ref, rwt_ref, rb_ref,
                  su_ref, x1_ref, h2_ref, idx_ref, wt_ref, rank_ref, cnt_ref, carry_s):
    @pl.when(pl.program_id(0) == 0)
    def _():
        carry_s[...] = jnp.zeros_like(carry_s)

    tm = x_ref.shape[0]
    a = _mm(ya_ref[...], wa_ref[...])
    b = _mm(yb_ref[...], wb_ref[...])
    merged = _sigmoid(ga_ref[...].astype(F32)) * a + _sigmoid(gb_ref[...].astype(F32)) * b
    m = mod_ref[0]
    x1 = x_ref[...] + m[2:3] * _mm(merged.astype(BF16), wo_ref[...])
    x1_ref[...] = x1
    y = x1 * lax.rsqrt(jnp.mean(x1 * x1, axis=-1, keepdims=True) + EPS) * n2_ref[...]
    h2 = y * (1.0 + m[4:5]) + m[3:4]
    h2_ref[...] = h2

    logits = _nt(rwt_ref[...], h2, precision=HIGHEST) + rb_ref[...]
    ie = lax.broadcasted_iota(I32, (N_EXPERTS, tm), 0).astype(F32)
    idxs, vals = [], []
    for _ in range(TOP_K):
        mx = jnp.max(logits, axis=0, keepdims=True)
        am = jnp.min(jnp.where(logits == mx, ie, float(N_EXPERTS)), axis=0, keepdims=True)
        idxs.append(am)
        vals.append(mx)
        logits = jnp.where(ie == am, -jnp.inf, logits)
    exps = [jnp.exp(v - vals[0]) for v in vals]
    inv = 1.0 / (exps[0] + exps[1] + exps[2] + exps[3])
    wt_ref[...] = jnp.concatenate([e * inv for e in exps], axis=0)
    idx_ref[...] = jnp.concatenate(idxs, axis=0).astype(I32)

    onehot = jnp.zeros((N_EXPERTS, tm), F32)
    for am in idxs:
        onehot = onehot + (ie == am).astype(F32)
    before = _mm(onehot.astype(BF16), su_ref[...]) + carry_s[:, 0:1]
    ranks = [jnp.sum(jnp.where(ie == am, before, 0.0), axis=0, keepdims=True) for am in idxs]
    rank_ref[...] = jnp.concatenate(ranks, axis=0).astype(I32)
    carry_s[...] = carry_s[...] + jnp.sum(onehot, axis=1, keepdims=True)
    cnt_ref[...] = carry_s[...]


def _merge(ya, yb, proj, x2, mod3, w_a, w_b, w_o, norm2_g, router_wt, router_b, su, seq):
    n_tok = x2.shape[0]
    tm = min(ROW_TILE_MERGE, seq)
    d = D_MODEL
    gcol = 7 * d // d
    row = lambda j: pl.BlockSpec((tm, d), lambda i: (i, j))
    const = lambda shape: pl.BlockSpec(shape, lambda i: tuple(0 for _ in shape))
    tok4 = pl.BlockSpec((TOP_K, tm), lambda i: (0, i))
    return pl.pallas_call(
        _merge_kernel,
        grid=(n_tok // tm,),
        in_specs=[
            row(0), row(0), row(gcol), row(gcol + 1), row(0),
            pl.BlockSpec((1, N_MOD, d), lambda i: ((i * tm) // seq, 0, 0)),
            const((d, d)), const((d, d)), const((d, d)), const((1, d)),
            const((N_EXPERTS, d)), const((N_EXPERTS, 1)), const((tm, tm)),
        ],
        out_specs=[
            row(0), row(0),
            tok4, tok4, tok4,
            const((N_EXPERTS, LANES)),
        ],
        out_shape=[
            jax.ShapeDtypeStruct((n_tok, d), F32),
            jax.ShapeDtypeStruct((n_tok, d), F32),
            jax.ShapeDtypeStruct((TOP_K, n_tok), I32),
            jax.ShapeDtypeStruct((TOP_K, n_tok), F32),
            jax.ShapeDtypeStruct((TOP_K, n_tok), I32),
            jax.ShapeDtypeStruct((N_EXPERTS, LANES), F32),
        ],
        scratch_shapes=[pltpu.VMEM((N_EXPERTS, LANES), F32)],
        compiler_params=pltpu.CompilerParams(dimension_semantics=("arbitrary",), vmem_limit_bytes=VMEM_LIMIT),
        name="merge",
    )(ya, yb, proj, proj, x2, mod3, w_a, w_b, w_o, norm2_g.reshape(1, d), router_wt, router_b.reshape(-1, 1), su)


def _scatter_kernel(fill_lo_ref, fill_hi_ref, nb_ref, dest_ref, h2_ref, xs_ref, zero_s, sem, zsem):
    tm = h2_ref.shape[0]
    bm = zero_s.shape[0]

    @pl.when(pl.program_id(0) == 0)
    def _():
        zero_s[...] = jnp.zeros_like(zero_s)

        def per_expert(e, total):
            lo, hi = fill_lo_ref[e], fill_hi_ref[e]

            def fill(r, c):
                pltpu.make_async_copy(zero_s.at[pl.ds(0, 1), :], xs_ref.at[pl.ds(r, 1), :], sem).start()
                return c

            lax.fori_loop(lo, hi, fill, 0)
            return total + (hi - lo)

        total = lax.fori_loop(0, N_EXPERTS, per_expert, 0)

        def drain(r, c):
            pltpu.make_async_copy(zero_s.at[pl.ds(0, 1), :], xs_ref.at[pl.ds(0, 1), :], sem).wait()
            return c

        lax.fori_loop(0, total, drain, 0)

        n_blocks = xs_ref.shape[0] // bm

        def fill_block(j, c):
            pltpu.make_async_copy(zero_s, xs_ref.at[pl.ds(pl.multiple_of(j * bm, bm), bm), :], zsem).start()
            return c

        lax.fori_loop(nb_ref[0], n_blocks, fill_block, 0)

        def drain_block(j, c):
            pltpu.make_async_copy(zero_s, xs_ref.at[pl.ds(0, bm), :], zsem).wait()
            return c

        lax.fori_loop(nb_ref[0], n_blocks, drain_block, 0)

    def send(g, c):
        for u in range(DMA_UNROLL):
            t = g * DMA_UNROLL + u
            for r in range(TOP_K):
                pltpu.make_async_copy(
                    h2_ref.at[pl.ds(t, 1), :], xs_ref.at[pl.ds(dest_ref[r, t], 1), :], sem).start()
        return c

    lax.fori_loop(0, tm // DMA_UNROLL, send, 0)

    def drain(g, c):
        for _ in range(DMA_UNROLL * TOP_K):
            pltpu.make_async_copy(h2_ref.at[pl.ds(0, 1), :], xs_ref.at[pl.ds(0, 1), :], sem).wait()
        return c

    lax.fori_loop(0, tm // DMA_UNROLL, drain, 0)


def _scatter(fill_lo, fill_hi, n_used, dest, h2s, n_rows, seq):
    n_tok = h2s.shape[0]
    tm = min(ROW_TILE_MOE, seq)
    smem = lambda: pl.BlockSpec(memory_space=pltpu.SMEM)
    tok4 = pl.BlockSpec((TOP_K, tm), lambda i: (0, i), memory_space=pltpu.SMEM)
    return pl.pallas_call(
        _scatter_kernel,
        grid=(n_tok // tm,),
        in_specs=[smem(), smem(), smem(), tok4, pl.BlockSpec((tm, D_MODEL), lambda i: (i, 0))],
        out_specs=pl.BlockSpec(memory_space=pl.ANY),
        out_shape=jax.ShapeDtypeStruct((n_rows, D_MODEL), F32),
        scratch_shapes=[pltpu.VMEM((ROW_TILE_MOE, D_MODEL), F32), pltpu.SemaphoreType.DMA,
                        pltpu.SemaphoreType.DMA],
        compiler_params=pltpu.CompilerParams(dimension_semantics=("arbitrary",)),
        name="scatter",
    )(fill_lo, fill_hi, n_used, dest, h2s)


def _expert_kernel(be_ref, nb_ref, xs_ref, w1_ref, b1_ref, w2_ref, b2_ref, ys_ref, w1_s, w2_s):
    j = pl.program_id(0)
    changed = jnp.logical_or(j == 0, be_ref[j] != be_ref[jnp.maximum(j - 1, 0)])

    @pl.when(changed)
    def _():
        w1_s[...] = w1_ref[...].astype(BF16)
        w2_s[...] = w2_ref[...].astype(BF16)

    @pl.when(j < nb_ref[0])
    def _():
        a = _mm(xs_ref[...].astype(BF16), w1_s[...]) + b1_ref[...]
        glu = jnp.minimum(a[:, :D_FF], SWIGLU_LIMIT)
        lin = jnp.clip(a[:, D_FF:], -SWIGLU_LIMIT, SWIGLU_LIMIT)
        hmid = (lin + 1.0) * (glu * _sigmoid(SWIGLU_ALPHA * glu))
        ys_ref[...] = _mm(hmid.astype(BF16), w2_s[...]) + b2_ref[...]

    @pl.when(j >= nb_ref[0])
    def _():
        ys_ref[...] = jnp.zeros_like(ys_ref)


def _experts(block_e, n_blocks_used, xs, w1, b1, w2, b2):
    n_rows = xs.shape[0]
    bm = ROW_TILE_MOE
    d = D_MODEL
    blk = lambda j, be, nb: (j, 0)
    exp = lambda j, be, nb: (be[j], 0, 0)
    return pl.pallas_call(
        _expert_kernel,
        grid_spec=pltpu.PrefetchScalarGridSpec(
            num_scalar_prefetch=2,
            grid=(n_rows // bm,),
            in_specs=[
                pl.BlockSpec((bm, d), blk),
                pl.BlockSpec((None, d, 2 * D_FF), exp),
                pl.BlockSpec((None, 1, 2 * D_FF), exp),
                pl.BlockSpec((None, D_FF, d), exp),
                pl.BlockSpec((None, 1, d), exp),
            ],
            out_specs=pl.BlockSpec((bm, d), blk),
            scratch_shapes=[pltpu.VMEM((d, 2 * D_FF), BF16), pltpu.VMEM((D_FF, d), BF16)],
        ),
        out_shape=jax.ShapeDtypeStruct((n_rows, d), F32),
        compiler_params=pltpu.CompilerParams(dimension_semantics=("arbitrary",), vmem_limit_bytes=VMEM_LIMIT),
        name="experts",
    )(block_e, n_blocks_used, xs, w1, b1.reshape(N_EXPERTS, 1, -1), w2, b2.reshape(N_EXPERTS, 1, -1))


def _combine_kernel(dest_ref, wt_ref, ys_ref, x1_ref, mod_ref, fg_ref, o_ref, buf_s, sem):
    tm = x1_ref.shape[0]

    def fetch(g, c):
        for u in range(DMA_UNROLL):
            t = g * DMA_UNROLL + u
            for r in range(TOP_K):
                pltpu.make_async_copy(
                    ys_ref.at[pl.ds(dest_ref[r, t], 1), :], buf_s.at[r, pl.ds(t, 1), :], sem).start()
        return c

    lax.fori_loop(0, tm // DMA_UNROLL, fetch, 0)

    def drain(g, c):
        for _ in range(DMA_UNROLL * TOP_K):
            pltpu.make_async_copy(ys_ref.at[pl.ds(0, 1), :], buf_s.at[0, pl.ds(0, 1), :], sem).wait()
        return c

    lax.fori_loop(0, tm // DMA_UNROLL, drain, 0)

    wt = wt_ref[...]
    moe = wt[:, 0:1] * buf_s[0]
    for r in range(1, TOP_K):
        moe = moe + wt[:, r:r + 1] * buf_s[r]
    x2 = x1_ref[...] + mod_ref[0][5:6] * moe
    o_ref[...] = x2 * lax.rsqrt(jnp.mean(x2 * x2, axis=-1, keepdims=True) + EPS) * fg_ref[...]


def _combine(dest, wt, ys, x1, mod3, final_g, seq):
    n_tok = x1.shape[0]
    tm = min(ROW_TILE_MOE, seq)
    d = D_MODEL
    tok4 = pl.BlockSpec((TOP_K, tm), lambda i: (0, i), memory_space=pltpu.SMEM)
    return pl.pallas_call(
        _combine_kernel,
        grid=(n_tok // tm,),
        in_specs=[
            tok4,
            pl.BlockSpec((tm, TOP_K), lambda i: (i, 0)),
            pl.BlockSpec(memory_space=pl.ANY),
            pl.BlockSpec((tm, d), lambda i: (i, 0)),
            pl.BlockSpec((1, N_MOD, d), lambda i: ((i * tm) // seq, 0, 0)),
            pl.BlockSpec((1, d), lambda i: (0, 0)),
        ],
        out_specs=pl.BlockSpec((tm, d), lambda i: (i, 0)),
        out_shape=jax.ShapeDtypeStruct((n_tok, d), F32),
        scratch_shapes=[pltpu.VMEM((TOP_K, tm, d), F32), pltpu.SemaphoreType.DMA],
        compiler_params=pltpu.CompilerParams(dimension_semantics=("arbitrary",), vmem_limit_bytes=VMEM_LIMIT),
        name="combine",
    )(dest, wt.T, ys, x1, mod3, final_g.reshape(1, d))


def _layer(x2, mod3, bsz, seq, norm1_g, w_in, conv_w, conv_b, ml_b_i, ml_b_f, ml_norm_g, w_branch_a, w_branch_b,
           w_out, norm2_g, router_w, router_b, expert_w1, expert_b1, expert_w2, expert_b2):
    d = D_MODEL
    n_tok = bsz * seq
    gate0 = 7 * d
    n_gate = 2 * ML_HEADS
    w_main = jnp.concatenate([w_in[:, :gate0], w_in[:, gate0 + n_gate:]], axis=1).astype(BF16)
    w_gate = jnp.pad(w_in[:, gate0:gate0 + n_gate], ((0, 0), (0, LANES - n_gate))).astype(BF16)
    gate_bias = jnp.pad(jnp.concatenate([ml_b_i, ml_b_f]), (0, LANES - n_gate)).reshape(1, LANES)

    proj, gates = _proj(x2, mod3, norm1_g, w_main, w_gate, seq)

    ones = jnp.ones((LANES, LANES), F32)
    u_half = jnp.concatenate([jnp.tril(ones), ones], axis=1)
    uu = jnp.concatenate([u_half, u_half], axis=0).astype(BF16)
    ya = _sb_attention(proj, uu, bsz, seq)

    cl = min(ML_CHUNK, seq)
    tri = jnp.tril(jnp.ones((cl, cl), F32))
    yb = _mlstm(proj, gates, gate_bias, conv_w, conv_b, ml_norm_g, tri, bsz, seq)

    tm = min(ROW_TILE_MERGE, seq)
    su = jnp.triu(jnp.ones((tm, tm), F32), k=1).astype(BF16)
    x1, h2s, idx, wt, rank, cnt = _merge(
        ya, yb, proj, x2, mod3, w_branch_a.astype(BF16), w_branch_b.astype(BF16), w_out.astype(BF16),
        norm2_g, router_w.T, router_b, su, seq)

    bm = ROW_TILE_MOE
    counts = cnt[:, 0].astype(I32)
    padded = (counts + bm - 1) // bm * bm
    pad_end = jnp.cumsum(padded)
    start = pad_end - padded
    n_rows = (n_tok * TOP_K + N_EXPERTS * (bm - 1)) // bm * bm
    n_blocks = n_rows // bm
    n_used = jnp.maximum(pad_end[-1] // bm, 1).astype(I32)
    blk = jnp.minimum(jnp.arange(n_blocks, dtype=I32), n_used - 1)
    block_e = jnp.minimum(jnp.sum((pad_end[None, :] <= (blk * bm)[:, None]).astype(I32), axis=1), N_EXPERTS - 1)

    onehot = idx[:, :, None] == jnp.arange(N_EXPERTS, dtype=I32)
    dest = rank + jnp.sum(jnp.where(onehot, start, 0), axis=-1)

    xs = _scatter(start + counts, pad_end, n_used.reshape(1), dest, h2s, n_rows, seq)
    ys = _experts(block_e, n_used.reshape(1), xs, expert_w1, expert_b1, expert_w2, expert_b2)
    return dest, wt, ys, x1


def kernel(x, c, ada_w, ada_b, norm1_g, w_in, conv_w, conv_b, ml_b_i, ml_b_f, ml_norm_g, w_branch_a, w_branch_b, w_out, norm2_g, router_w, router_b, expert_w1, expert_b1, expert_w2, expert_b2, final_g):
    bsz, seq, d = x.shape
    depth = ada_w.shape[0]
    assert d == D_MODEL and depth == 1 and seq % QUERY_TILE == 0
    x2 = x.reshape(bsz * seq, d)
    mod3 = _ada(c, ada_w[0], ada_b[0]).reshape(bsz, N_MOD, d)
    dest, wt, ys, x1 = _layer(
        x2, mod3, bsz, seq, norm1_g[0], w_in[0], conv_w[0], conv_b[0], ml_b_i[0], ml_b_f[0], ml_norm_g[0],
        w_branch_a[0], w_branch_b[0], w_out[0], norm2_g[0], router_w[0], router_b[0],
        expert_w1[0], expert_b1[0], expert_w2[0], expert_b2[0])
    out = _combine(dest, wt, ys, x1, mod3, final_g, seq)
    return out.reshape(bsz, seq, d)
```

```python
import functools

import jax
import jax.numpy as jnp
from jax import lax
from jax.experimental import pallas as pl
from jax.experimental.pallas import tpu as pltpu

F32 = jnp.float32
BF16 = jnp.bfloat16
I32 = jnp.int32
HIGHEST = lax.Precision.HIGHEST

D_MODEL = 1024
SB_HEAD_DIM = 64
ML_HEADS = 4
ML_HEAD_DIM = 256
CONV_WIDTH = 4
N_EXPERTS = 32
TOP_K = 4
D_FF = 1024
SWIGLU_LIMIT = 7.0
SWIGLU_ALPHA = 1.702
N_MOD = 6
EPS = 1e-6
LOG2E = 1.4426950408889634

LANES = 128
SUBLANES = 8
DMA_UNROLL = 8

QUERY_TILE = 128
SB_Q_SPAN = 512
SB_K_SPAN = 512
ML_CHUNK = 256
CONV_HIST = 16
ROW_TILE_PROJ = 1024
ROW_TILE_MERGE = 512
ROW_TILE_MOE = 256
MOE_TOKEN_TILE = 512
VMEM_LIMIT = 56 * 1024 * 1024


def _nt(a, b, precision=None):
    return lax.dot_general(a, b, (((1,), (1,)), ((), ())), preferred_element_type=F32, precision=precision)


def _mm(a, b, precision=None):
    return jnp.dot(a, b, preferred_element_type=F32, precision=precision)


def _sigmoid(x):
    return 1.0 / (1.0 + jnp.exp(-x))


def _log_sigmoid(x):
    return jnp.minimum(x, 0.0) - jnp.log(1.0 + jnp.exp(-jnp.abs(x)))


def _ada_kernel(c_ref, w_ref, b_ref, o_ref):
    c = c_ref[...]
    o_ref[...] = _mm(c * _sigmoid(c), w_ref[...], precision=HIGHEST) + b_ref[...]


def _ada(c, ada_w, ada_b):
    bsz = c.shape[0]
    n = ada_w.shape[1]
    return pl.pallas_call(
        _ada_kernel,
        grid=(n // D_MODEL,),
        in_specs=[
            pl.BlockSpec((bsz, D_MODEL), lambda j: (0, 0)),
            pl.BlockSpec((D_MODEL, D_MODEL), lambda j: (0, j)),
            pl.BlockSpec((1, D_MODEL), lambda j: (0, j)),
        ],
        out_specs=pl.BlockSpec((bsz, D_MODEL), lambda j: (0, j)),
        out_shape=jax.ShapeDtypeStruct((bsz, n), F32),
        name="ada",
    )(c, ada_w, ada_b.reshape(1, n))


def _proj_kernel(x_ref, mod_ref, g_ref, w_ref, wg_ref, o_ref, og_ref, h_ref):
    @pl.when(pl.program_id(1) == 0)
    def _():
        x = x_ref[...]
        y = x * lax.rsqrt(jnp.mean(x * x, axis=-1, keepdims=True) + EPS) * g_ref[...]
        m = mod_ref[0]
        hb = (y * (1.0 + m[1:2]) + m[0:1]).astype(BF16)
        h_ref[...] = hb
        og_ref[...] = _mm(hb, wg_ref[...])

    o_ref[...] = _mm(h_ref[...], w_ref[...]).astype(BF16)


def _proj(x2, mod3, norm_g, w_main, w_gate, seq):
    n_tok = x2.shape[0]
    n_main = w_main.shape[1]
    tm = min(ROW_TILE_PROJ, seq)
    tn = D_MODEL
    return pl.pallas_call(
        _proj_kernel,
        grid=(n_tok // tm, n_main // tn),
        in_specs=[
            pl.BlockSpec((tm, D_MODEL), lambda i, j: (i, 0)),
            pl.BlockSpec((1, N_MOD, D_MODEL), lambda i, j: ((i * tm) // seq, 0, 0)),
            pl.BlockSpec((1, D_MODEL), lambda i, j: (0, 0)),
            pl.BlockSpec((D_MODEL, tn), lambda i, j: (0, j)),
            pl.BlockSpec((D_MODEL, LANES), lambda i, j: (0, 0)),
        ],
        out_specs=[
            pl.BlockSpec((tm, tn), lambda i, j: (i, j)),
            pl.BlockSpec((tm, LANES), lambda i, j: (i, 0)),
        ],
        out_shape=[
            jax.ShapeDtypeStruct((n_tok, n_main), BF16),
            jax.ShapeDtypeStruct((n_tok, LANES), F32),
        ],
        scratch_shapes=[pltpu.VMEM((tm, D_MODEL), BF16)],
        compiler_params=pltpu.CompilerParams(
            dimension_semantics=("arbitrary", "arbitrary"), vmem_limit_bytes=VMEM_LIMIT),
        name="proj",
    )(x2, mod3, norm_g.reshape(1, D_MODEL), w_main, w_gate)


def _sb_kernel(q_ref, k_ref, v_ref, uu_ref, o_ref, qh_s, run_s, acc_s):
    seq = q_ref.shape[0]
    qt = QUERY_TILE
    tq, tk = min(SB_Q_SPAN, seq), min(SB_K_SPAN, seq)
    n_heads = LANES // SB_HEAD_DIM
    n_qb, n_kb = tq // qt, tk // qt
    lane = lax.broadcasted_iota(I32, (qt, LANES), 1)
    strict = lane < lax.broadcasted_iota(I32, (qt, LANES), 0)
    uu = uu_ref[...]

    def group(s0, diag):
        k = k_ref[pl.ds(s0, tk), :]
        v = v_ref[pl.ds(s0, tk), :]
        chains = range(n_qb * n_heads)
        nbs = [ci // n_heads + 1 if diag else n_kb for ci in chains]
        blk = lambda x, b: x[:, b * qt:(b + 1) * qt]
        zs = [_nt(qh_s[ci], k[:nbs[ci] * qt]) for ci in chains]
        lkb = []
        for ci in chains:
            z = zs[ci]
            nz = -z
            lk = jnp.minimum(nz, 0.0) - jnp.log(1.0 + jnp.exp2(jnp.minimum(z, nz))) * LOG2E
            lks = [blk(lk, b) for b in range(nbs[ci])]
            if diag:
                lks[-1] = jnp.where(strict, lks[-1], 0.0)
            lkb.append([x.astype(BF16) for x in lks])
        ccs = [[_mm(lkb[ci][b], uu) for b in range(nbs[ci])] for ci in chains]
        probs = []
        for ci in chains:
            run = run_s[ci]
            a = [None] * nbs[ci]
            for b in reversed(range(nbs[ci])):
                a[b] = jnp.exp2(blk(zs[ci], b) + ccs[ci][b][:, :LANES] + run)
                run = run + ccs[ci][b][:, LANES:]
            run_s[ci] = run
            if diag:
                a[-1] = jnp.where(strict, a[-1], 0.0)
            probs.append(jnp.concatenate(a, axis=1).astype(BF16))
        for ci in chains:
            acc_s[ci] = acc_s[ci] + _mm(probs[ci], v[:nbs[ci] * qt])

    def qspan(i, carry):
        r0 = pl.multiple_of(i * tq, tq)
        run_s[...] = jnp.zeros_like(run_s)
        acc_s[...] = jnp.zeros_like(acc_s)
        for qb in range(n_qb):
            qf = q_ref[pl.ds(r0 + qb * qt, qt), :].astype(F32) * (SB_HEAD_DIM ** -0.5 * LOG2E)
            for h in range(n_heads):
                in_head = (lane >= h * SB_HEAD_DIM) & (lane < (h + 1) * SB_HEAD_DIM)
                qh_s[qb * n_heads + h] = jnp.where(in_head, qf, 0.0).astype(BF16)

        group(r0, True)

        def past(j, c):
            group(pl.multiple_of((i - 1 - j) * tk, tk), False)
            return c

        lax.fori_loop(0, i, past, 0)
        for qb in range(n_qb):
            o_ref[pl.ds(r0 + qb * qt, qt), :] = jnp.where(
                lane < SB_HEAD_DIM, acc_s[qb * n_heads], acc_s[qb * n_heads + 1]).astype(BF16)
        return carry

    lax.fori_loop(0, seq // tq, qspan, 0)


def _sb_attention(proj, uu, bsz, seq):
    n_pairs = D_MODEL // LANES
    tq, tk = min(SB_Q_SPAN, seq), min(SB_K_SPAN, seq)
    assert tk == tq and seq % tk == 0 and LANES // SB_HEAD_DIM == 2
    n_chain = (tq // QUERY_TILE) * (LANES // SB_HEAD_DIM)
    return pl.pallas_call(
        _sb_kernel,
        grid=(bsz, n_pairs),
        in_specs=[
            pl.BlockSpec((seq, LANES), lambda b, p: (b, p)),
            pl.BlockSpec((seq, LANES), lambda b, p: (b, n_pairs + p)),
            pl.BlockSpec((seq, LANES), lambda b, p: (b, 2 * n_pairs + p)),
            pl.BlockSpec((LANES, 2 * LANES), lambda b, p: (0, 0)),
        ],
        out_specs=pl.BlockSpec((seq, LANES), lambda b, p: (b, p)),
        out_shape=jax.ShapeDtypeStruct((bsz * seq, D_MODEL), BF16),
        scratch_shapes=[
            pltpu.VMEM((n_chain, QUERY_TILE, LANES), BF16),
            pltpu.VMEM((n_chain, QUERY_TILE, LANES), F32),
            pltpu.VMEM((n_chain, QUERY_TILE, LANES), F32),
        ],
        compiler_params=pltpu.CompilerParams(dimension_semantics=("arbitrary", "arbitrary")),
        name="sb_attn",
    )(proj, proj, proj, uu)


def _mlstm_kernel(q_ref, k_ref, v_ref, og_ref, gt_ref, gb_ref, cwq_ref, cwk_ref, cbq_ref, cbk_ref, ng_ref, tri_ref,
                  sh_ref, y_ref, qc_s, kc_s, c_s, n_s, m_s):
    head = pl.program_id(1)
    seq = q_ref.shape[0]
    cl = min(ML_CHUNK, seq)
    c_s[...] = jnp.zeros_like(c_s)
    n_s[...] = jnp.zeros_like(n_s)
    m_s[...] = jnp.zeros_like(m_s)
    lane = lax.broadcasted_iota(I32, (cl, LANES), 1)
    sel_i = (lane == head).astype(F32)
    sel_f = (lane == head + ML_HEADS).astype(F32)
    tril = lax.broadcasted_iota(I32, (cl, cl), 1) <= lax.broadcasted_iota(I32, (cl, cl), 0)

    row8 = lax.broadcasted_iota(I32, (SUBLANES, ML_HEAD_DIM), 0)

    def conv_silu(x_ref, cw_ref, cb_ref, c, r0):
        cur = x_ref[pl.ds(r0, cl), :]
        hist = x_ref[pl.ds(pl.multiple_of(jnp.maximum(r0 - CONV_HIST, 0), CONV_HIST), CONV_HIST), :]
        hist = jnp.where(c > 0, hist.astype(F32), 0.0)[CONV_HIST - SUBLANES:, :]
        y = cb_ref[...] + cur.astype(F32) * cw_ref[CONV_WIDTH - 1:CONV_WIDTH, :]
        head_fix = jnp.zeros((SUBLANES, ML_HEAD_DIM), F32)
        for tap in range(CONV_WIDTH - 1):
            back = CONV_WIDTH - 1 - tap
            w = cw_ref[tap:tap + 1, :]
            y = y + _mm(sh_ref[tap], cur) * w
            head_fix = head_fix + jnp.where(row8 < back, pltpu.roll(hist, back, axis=0), 0.0) * w
        y = jnp.concatenate([y[:SUBLANES] + head_fix, y[SUBLANES:]], axis=0)
        return y * _sigmoid(y)

    def conv_qk(c):
        r0 = pl.multiple_of(c * cl, cl)
        qc_s[...] = conv_silu(q_ref, cwq_ref, cbq_ref, c, r0)
        kc_s[...] = conv_silu(k_ref, cwk_ref, cbk_ref, c, r0) * (ML_HEAD_DIM ** -0.5)

    conv_qk(0)

    def chunk(c, carry):
        r0 = pl.multiple_of(c * cl, cl)
        qc = qc_s[...]
        kc = kc_s[...]
        qb = qc.astype(BF16)
        kb = kc.astype(BF16)
        vb = v_ref[pl.ds(r0, cl), :]
        conv_qk(jnp.minimum(c + 1, seq // cl - 1))

        pre = gt_ref[pl.ds(r0, cl), :] + gb_ref[...]
        bt = _mm(tri_ref[...], _log_sigmoid(pre), precision=HIGHEST)
        b_col = jnp.sum(bt * sel_f, axis=1, keepdims=True)
        i_col = jnp.sum(pre * sel_i, axis=1, keepdims=True)
        g_col = i_col - b_col
        g_row = jnp.transpose(jnp.broadcast_to(g_col, (cl, LANES)))[0:1, :]

        m_prev = m_s[...]
        dmat = jnp.where(tril, b_col + g_row, -jnp.inf)
        m_row = jnp.maximum(b_col + m_prev, jnp.max(dmat, axis=1, keepdims=True))
        w_inter = jnp.exp(b_col + m_prev - m_row)
        s_mat = _nt(qb, kb) * jnp.exp(dmat - m_row)
        num = w_inter * _mm(qb, c_s[...].astype(BF16)) + _mm(s_mat.astype(BF16), vb)
        den = w_inter * jnp.sum(qc * n_s[...], axis=1, keepdims=True) + jnp.sum(s_mat, axis=1, keepdims=True)
        hh = num * (1.0 / jnp.maximum(jnp.abs(den), jnp.exp(-m_row)))

        m_new = m_row[cl - 1:cl, :]
        b_last = b_col[cl - 1:cl, :]
        decay = jnp.exp(b_last + m_prev - m_new)
        wk = jnp.exp(b_last + g_col - m_new) * kc
        c_s[...] = decay * c_s[...] + lax.dot_general(
            wk.astype(BF16), vb, (((0,), (0,)), ((), ())), preferred_element_type=F32)
        n_s[...] = decay * n_s[...] + jnp.sum(wk, axis=0, keepdims=True)
        m_s[...] = m_new

        hn = hh * lax.rsqrt(jnp.mean(hh * hh, axis=1, keepdims=True) + EPS) * ng_ref[...]
        y_ref[pl.ds(r0, cl), :] = (hn * _sigmoid(og_ref[pl.ds(r0, cl), :].astype(F32))).astype(BF16)
        return carry

    lax.fori_loop(0, seq // cl, chunk, 0)


def _mlstm(proj, gates, gate_bias, conv_w, conv_b, norm_g, tri, bsz, seq):
    hd = ML_HEAD_DIM
    cl = min(ML_CHUNK, seq)
    col0 = 3 * D_MODEL // hd
    row_spec = lambda off: pl.BlockSpec((seq, hd), lambda b, h: (b, off + h))
    t_idx = jnp.arange(cl)[:, None]
    j_idx = jnp.arange(cl)[None, :]
    shifts = jnp.stack([(j_idx == t_idx + tap - (CONV_WIDTH - 1)) for tap in range(CONV_WIDTH - 1)]).astype(BF16)
    return pl.pallas_call(
        _mlstm_kernel,
        grid=(bsz, ML_HEADS),
        in_specs=[
            row_spec(col0), row_spec(col0 + ML_HEADS), row_spec(col0 + 2 * ML_HEADS), row_spec(col0 + 3 * ML_HEADS),
            pl.BlockSpec((seq, LANES), lambda b, h: (b, 0)),
            pl.BlockSpec((1, LANES), lambda b, h: (0, 0)),
            pl.BlockSpec((CONV_WIDTH, hd), lambda b, h: (0, h)),
            pl.BlockSpec((CONV_WIDTH, hd), lambda b, h: (0, ML_HEADS + h)),
            pl.BlockSpec((1, hd), lambda b, h: (0, h)),
            pl.BlockSpec((1, hd), lambda b, h: (0, ML_HEADS + h)),
            pl.BlockSpec((1, hd), lambda b, h: (0, h)),
            pl.BlockSpec((cl, cl), lambda b, h: (0, 0)),
            pl.BlockSpec((CONV_WIDTH - 1, cl, cl), lambda b, h: (0, 0, 0)),
        ],
        out_specs=pl.BlockSpec((seq, hd), lambda b, h: (b, h)),
        out_shape=jax.ShapeDtypeStruct((bsz * seq, D_MODEL), BF16),
        scratch_shapes=[
            pltpu.VMEM((cl, hd), F32),
            pltpu.VMEM((cl, hd), F32),
            pltpu.VMEM((hd, hd), F32),
            pltpu.VMEM((1, hd), F32),
            pltpu.VMEM((1, 1), F32),
        ],
        compiler_params=pltpu.CompilerParams(
            dimension_semantics=("arbitrary", "arbitrary"), vmem_limit_bytes=VMEM_LIMIT),
        name="mlstm",
    )(proj, proj, proj, proj, gates, gate_bias, conv_w, conv_w,
      conv_b.reshape(1, -1), conv_b.reshape(1, -1), norm_g.reshape(1, -1), tri, shifts)


def _merge_kernel(ya_ref, yb_ref, ga_ref, gb_ref, x_ref, mod_ref, wa_ref, wb_ref, wo_ref, n2_ref, rwt_ref, rb_ref,
                  su_ref, x1_ref, h2_ref, idx_ref, wt_ref, rank_ref, cnt_ref, carry_s):
    @pl.when(pl.program_id(0) == 0)
    def _():
        carry_s[...] = jnp.zeros_like(carry_s)

    tm = x_ref.shape[0]
    a = _mm(ya_ref[...], wa_ref[...])
    b = _mm(yb_ref[...], wb_ref[...])
    merged = _sigmoid(ga_ref[...].astype(F32)) * a + _sigmoid(gb_ref[...].astype(F32)) * b
    m = mod_ref[0]
    x1 = x_ref[...] + m[2:3] * _mm(merged.astype(BF16), wo_ref[...])
    x1_ref[...] = x1
    y = x1 * lax.rsqrt(jnp.mean(x1 * x1, axis=-1, keepdims=True) + EPS) * n2_ref[...]
    h2 = y * (1.0 + m[4:5]) + m[3:4]
    h2_ref[...] = h2

    logits = _nt(rwt_ref[...], h2, precision=HIGHEST) + rb_ref[...]
    ie = lax.broadcasted_iota(I32, (N_EXPERTS, tm), 0).astype(F32)
    idxs, vals = [], []
    for _ in range(TOP_K):
        mx = jnp.max(logits, axis=0, keepdims=True)
        am = jnp.min(jnp.where(logits == mx, ie, float(N_EXPERTS)), axis=0, keepdims=True)
        idxs.append(am)
        vals.append(mx)
        logits = jnp.where(ie == am, -jnp.inf, logits)
    exps = [jnp.exp(v - vals[0]) for v in vals]
    inv = 1.0 / (exps[0] + exps[1] + exps[2] + exps[3])
    wt_ref[...] = jnp.concatenate([e * inv for e in exps], axis=0)
    idx_ref[...] = jnp.concatenate(idxs, axis=0).astype(I32)

    onehot = jnp.zeros((N_EXPERTS, tm), F32)
    for am in idxs:
        onehot = onehot + (ie == am).astype(F32)
    before = _mm(onehot.astype(BF16), su_ref[...]) + carry_s[:, 0:1]
    ranks = [jnp.sum(jnp.where(ie == am, before, 0.0), axis=0, keepdims=True) for am in idxs]
    rank_ref[...] = jnp.concatenate(ranks, axis=0).astype(I32)
    carry_s[...] = carry_s[...] + jnp.sum(onehot, axis=1, keepdims=True)
    cnt_ref[...] = carry_s[...]


def _merge(ya, yb, proj, x2, mod3, w_a, w_b, w_o, norm2_g, router_wt, router_b, su, seq):
    n_tok = x2.shape[0]
    tm = min(ROW_TILE_MERGE, seq)
    d = D_MODEL
    gcol = 7 * d // d
    row = lambda j: pl.BlockSpec((tm, d), lambda i: (i, j))
    const = lambda shape: pl.BlockSpec(shape, lambda i: tuple(0 for _ in shape))
    tok4 = pl.BlockSpec((TOP_K, tm), lambda i: (0, i))
    return pl.pallas_call(
        _merge_kernel,
        grid=(n_tok // tm,),
        in_specs=[
            row(0), row(0), row(gcol), row(gcol + 1), row(0),
            pl.BlockSpec((1, N_MOD, d), lambda i: ((i * tm) // seq, 0, 0)),
            const((d, d)), const((d, d)), const((d, d)), const((1, d)),
            const((N_EXPERTS, d)), const((N_EXPERTS, 1)), const((tm, tm)),
        ],
        out_specs=[
            row(0), row(0),
            tok4, tok4, tok4,
            const((N_EXPERTS, LANES)),
        ],
        out_shape=[
            jax.ShapeDtypeStruct((n_tok, d), F32),
            jax.ShapeDtypeStruct((n_tok, d), F32),
            jax.ShapeDtypeStruct((TOP_K, n_tok), I32),
            jax.ShapeDtypeStruct((TOP_K, n_tok), F32),
            jax.ShapeDtypeStruct((TOP_K, n_tok), I32),
            jax.ShapeDtypeStruct((N_EXPERTS, LANES), F32),
        ],
        scratch_shapes=[pltpu.VMEM((N_EXPERTS, LANES), F32)],
        compiler_params=pltpu.CompilerParams(dimension_semantics=("arbitrary",), vmem_limit_bytes=VMEM_LIMIT),
        name="merge",
    )(ya, yb, proj, proj, x2, mod3, w_a, w_b, w_o, norm2_g.reshape(1, d), router_wt, router_b.reshape(-1, 1), su)


def _scatter_kernel(fill_lo_ref, fill_hi_ref, nb_ref, dest_ref, h2_ref, xs_ref, zero_s, sem, zsem):
    tm = h2_ref.shape[0]
    bm = zero_s.shape[0]

    @pl.when(pl.program_id(0) == 0)
    def _():
        zero_s[...] = jnp.zeros_like(zero_s)

        def per_expert(e, total):
            lo, hi = fill_lo_ref[e], fill_hi_ref[e]

            def fill(r, c):
                pltpu.make_async_copy(zero_s.at[pl.ds(0, 1), :], xs_ref.at[pl.ds(r, 1), :], sem).start()
                return c

            lax.fori_loop(lo, hi, fill, 0)
            return total + (hi - lo)

        total = lax.fori_loop(0, N_EXPERTS, per_expert, 0)

        def drain(r, c):
            pltpu.make_async_copy(zero_s.at[pl.ds(0, 1), :], xs_ref.at[pl.ds(0, 1), :], sem).wait()
            return c

        lax.fori_loop(0, total, drain, 0)

        n_blocks = xs_ref.shape[0] // bm

        def fill_block(j, c):
            pltpu.make_async_copy(zero_s, xs_ref.at[pl.ds(pl.multiple_of(j * bm, bm), bm), :], zsem).start()
            return c

        lax.fori_loop(nb_ref[0], n_blocks, fill_block, 0)

        def drain_block(j, c):
            pltpu.make_async_copy(zero_s, xs_ref.at[pl.ds(0, bm), :], zsem).wait()
            return c

        lax.fori_loop(nb_ref[0], n_blocks, drain_block, 0)

    def send(g, c):
        for u in range(DMA_UNROLL):
            t = g * DMA_UNROLL + u
            for r in range(TOP_K):
                pltpu.make_async_copy(
                    h2_ref.at[pl.ds(t, 1), :], xs_ref.at[pl.ds(dest_ref[r, t], 1), :], sem).start()
        return c

    lax.fori_loop(0, tm // DMA_UNROLL, send, 0)

    def drain(g, c):
        for _ in range(DMA_UNROLL * TOP_K):
            pltpu.make_async_copy(h2_ref.at[pl.ds(0, 1), :], xs_ref.at[pl.ds(0, 1), :], sem).wait()
        return c

    lax.fori_loop(0, tm // DMA_UNROLL, drain, 0)


def _scatter(fill_lo, fill_hi, n_used, dest, h2s, n_rows, seq):
    n_tok = h2s.shape[0]
    tm = min(MOE_TOKEN_TILE, seq)
    smem = lambda: pl.BlockSpec(memory_space=pltpu.SMEM)
    tok4 = pl.BlockSpec((TOP_K, tm), lambda i: (0, i), memory_space=pltpu.SMEM)
    return pl.pallas_call(
        _scatter_kernel,
        grid=(n_tok // tm,),
        in_specs=[smem(), smem(), smem(), tok4, pl.BlockSpec((tm, D_MODEL), lambda i: (i, 0))],
        out_specs=pl.BlockSpec(memory_space=pl.ANY),
        out_shape=jax.ShapeDtypeStruct((n_rows, D_MODEL), F32),
        scratch_shapes=[pltpu.VMEM((ROW_TILE_MOE, D_MODEL), F32), pltpu.SemaphoreType.DMA,
                        pltpu.SemaphoreType.DMA],
        compiler_params=pltpu.CompilerParams(dimension_semantics=("arbitrary",)),
        name="scatter",
    )(fill_lo, fill_hi, n_used, dest, h2s)


def _expert_kernel(be_ref, nb_ref, xs_ref, w1_ref, b1_ref, w2_ref, b2_ref, ys_ref, w1_s, w2_s):
    j = pl.program_id(0)
    changed = jnp.logical_or(j == 0, be_ref[j] != be_ref[jnp.maximum(j - 1, 0)])

    @pl.when(changed)
    def _():
        w1_s[...] = w1_ref[...].astype(BF16)
        w2_s[...] = w2_ref[...].astype(BF16)

    @pl.when(j < nb_ref[0])
    def _():
        a = _mm(xs_ref[...].astype(BF16), w1_s[...]) + b1_ref[...]
        glu = jnp.minimum(a[:, :D_FF], SWIGLU_LIMIT)
        lin = jnp.clip(a[:, D_FF:], -SWIGLU_LIMIT, SWIGLU_LIMIT)
        hmid = (lin + 1.0) * (glu * _sigmoid(SWIGLU_ALPHA * glu))
        ys_ref[...] = _mm(hmid.astype(BF16), w2_s[...]) + b2_ref[...]

    @pl.when(j >= nb_ref[0])
    def _():
        ys_ref[...] = jnp.zeros_like(ys_ref)


def _experts(block_e, n_blocks_used, xs, w1, b1, w2, b2):
    n_rows = xs.shape[0]
    bm = ROW_TILE_MOE
    d = D_MODEL
    blk = lambda j, be, nb: (j, 0)
    exp = lambda j, be, nb: (be[j], 0, 0)
    return pl.pallas_call(
        _expert_kernel,
        grid_spec=pltpu.PrefetchScalarGridSpec(
            num_scalar_prefetch=2,
            grid=(n_rows // bm,),
            in_specs=[
                pl.BlockSpec((bm, d), blk),
                pl.BlockSpec((None, d, 2 * D_FF), exp),
                pl.BlockSpec((None, 1, 2 * D_FF), exp),
                pl.BlockSpec((None, D_FF, d), exp),
                pl.BlockSpec((None, 1, d), exp),
            ],
            out_specs=pl.BlockSpec((bm, d), blk),
            scratch_shapes=[pltpu.VMEM((d, 2 * D_FF), BF16), pltpu.VMEM((D_FF, d), BF16)],
        ),
        out_shape=jax.ShapeDtypeStruct((n_rows, d), F32),
        compiler_params=pltpu.CompilerParams(dimension_semantics=("arbitrary",), vmem_limit_bytes=VMEM_LIMIT),
        name="experts",
    )(block_e, n_blocks_used, xs, w1, b1.reshape(N_EXPERTS, 1, -1), w2, b2.reshape(N_EXPERTS, 1, -1))


def _combine_kernel(dest_ref, wt_ref, ys_ref, x1_ref, mod_ref, fg_ref, o_ref, buf_s, sem):
    tm = x1_ref.shape[0]

    def fetch(g, c):
        for u in range(DMA_UNROLL):
            t = g * DMA_UNROLL + u
            for r in range(TOP_K):
                pltpu.make_async_copy(
                    ys_ref.at[pl.ds(dest_ref[r, t], 1), :], buf_s.at[r, pl.ds(t, 1), :], sem).start()
        return c

    lax.fori_loop(0, tm // DMA_UNROLL, fetch, 0)

    def drain(g, c):
        for _ in range(DMA_UNROLL * TOP_K):
            pltpu.make_async_copy(ys_ref.at[pl.ds(0, 1), :], buf_s.at[0, pl.ds(0, 1), :], sem).wait()
        return c

    lax.fori_loop(0, tm // DMA_UNROLL, drain, 0)

    wt = wt_ref[...]
    moe = wt[:, 0:1] * buf_s[0]
    for r in range(1, TOP_K):
        moe = moe + wt[:, r:r + 1] * buf_s[r]
    x2 = x1_ref[...] + mod_ref[0][5:6] * moe
    o_ref[...] = x2 * lax.rsqrt(jnp.mean(x2 * x2, axis=-1, keepdims=True) + EPS) * fg_ref[...]


def _combine(dest, wt, ys, x1, mod3, final_g, seq):
    n_tok = x1.shape[0]
    tm = min(MOE_TOKEN_TILE, seq)
    d = D_MODEL
    tok4 = pl.BlockSpec((TOP_K, tm), lambda i: (0, i), memory_space=pltpu.SMEM)
    return pl.pallas_call(
        _combine_kernel,
        grid=(n_tok // tm,),
        in_specs=[
            tok4,
            pl.BlockSpec((tm, TOP_K), lambda i: (i, 0)),
            pl.BlockSpec(memory_space=pl.ANY),
            pl.BlockSpec((tm, d), lambda i: (i, 0)),
            pl.BlockSpec((1, N_MOD, d), lambda i: ((i * tm) // seq, 0, 0)),
            pl.BlockSpec((1, d), lambda i: (0, 0)),
        ],
        out_specs=pl.BlockSpec((tm, d), lambda i: (i, 0)),
        out_shape=jax.ShapeDtypeStruct((n_tok, d), F32),
        scratch_shapes=[pltpu.VMEM((TOP_K, tm, d), F32), pltpu.SemaphoreType.DMA],
        compiler_params=pltpu.CompilerParams(dimension_semantics=("arbitrary",), vmem_limit_bytes=VMEM_LIMIT),
        name="combine",
    )(dest, wt.T, ys, x1, mod3, final_g.reshape(1, d))


def _layer(x2, mod3, bsz, seq, norm1_g, w_in, conv_w, conv_b, ml_b_i, ml_b_f, ml_norm_g, w_branch_a, w_branch_b,
           w_out, norm2_g, router_w, router_b, expert_w1, expert_b1, expert_w2, expert_b2):
    d = D_MODEL
    n_tok = bsz * seq
    gate0 = 7 * d
    n_gate = 2 * ML_HEADS
    w_main = jnp.concatenate([w_in[:, :gate0], w_in[:, gate0 + n_gate:]], axis=1).astype(BF16)
    w_gate = jnp.pad(w_in[:, gate0:gate0 + n_gate], ((0, 0), (0, LANES - n_gate))).astype(BF16)
    gate_bias = jnp.pad(jnp.concatenate([ml_b_i, ml_b_f]), (0, LANES - n_gate)).reshape(1, LANES)

    proj, gates = _proj(x2, mod3, norm1_g, w_main, w_gate, seq)

    ones = jnp.ones((LANES, LANES), F32)
    uu = jnp.concatenate([jnp.tril(ones), ones], axis=1).astype(BF16)
    ya = _sb_attention(proj, uu, bsz, seq)

    cl = min(ML_CHUNK, seq)
    tri = jnp.tril(jnp.ones((cl, cl), F32))
    yb = _mlstm(proj, gates, gate_bias, conv_w, conv_b, ml_norm_g, tri, bsz, seq)

    tm = min(ROW_TILE_MERGE, seq)
    su = jnp.triu(jnp.ones((tm, tm), F32), k=1).astype(BF16)
    x1, h2s, idx, wt, rank, cnt = _merge(
        ya, yb, proj, x2, mod3, w_branch_a.astype(BF16), w_branch_b.astype(BF16), w_out.astype(BF16),
        norm2_g, router_w.T, router_b, su, seq)

    bm = ROW_TILE_MOE
    counts = cnt[:, 0].astype(I32)
    padded = (counts + bm - 1) // bm * bm
    pad_end = jnp.cumsum(padded)
    start = pad_end - padded
    n_rows = (n_tok * TOP_K + N_EXPERTS * (bm - 1)) // bm * bm
    n_blocks = n_rows // bm
    n_used = jnp.maximum(pad_end[-1] // bm, 1).astype(I32)
    blk = jnp.minimum(jnp.arange(n_blocks, dtype=I32), n_used - 1)
    block_e = jnp.minimum(jnp.sum((pad_end[None, :] <= (blk * bm)[:, None]).astype(I32), axis=1), N_EXPERTS - 1)

    onehot = idx[:, :, None] == jnp.arange(N_EXPERTS, dtype=I32)
    dest = rank + jnp.sum(jnp.where(onehot, start, 0), axis=-1)

    xs = _scatter(start + counts, pad_end, n_used.reshape(1), dest, h2s, n_rows, seq)
    ys = _experts(block_e, n_used.reshape(1), xs, expert_w1, expert_b1, expert_w2, expert_b2)
    return dest, wt, ys, x1


def kernel(x, c, ada_w, ada_b, norm1_g, w_in, conv_w, conv_b, ml_b_i, ml_b_f, ml_norm_g, w_branch_a, w_branch_b, w_out, norm2_g, router_w, router_b, expert_w1, expert_b1, expert_w2, expert_b2, final_g):
    bsz, seq, d = x.shape
    depth = ada_w.shape[0]
    assert d == D_MODEL and depth == 1 and seq % QUERY_TILE == 0
    x2 = x.reshape(bsz * seq, d)
    mod3 = _ada(c, ada_w[0], ada_b[0]).reshape(bsz, N_MOD, d)
    dest, wt, ys, x1 = _layer(
        x2, mod3, bsz, seq, norm1_g[0], w_in[0], conv_w[0], conv_b[0], ml_b_i[0], ml_b_f[0], ml_norm_g[0],
        w_branch_a[0], w_branch_b[0], w_out[0], norm2_g[0], router_w[0], router_b[0],
        expert_w1[0], expert_b1[0], expert_w2[0], expert_b2[0])
    out = _combine(dest, wt, ys, x1, mod3, final_g, seq)
    return out.reshape(bsz, seq, d)
```

```python
import functools

import jax
import jax.numpy as jnp
from jax import lax
from jax.experimental import pallas as pl
from jax.experimental.pallas import tpu as pltpu

F32 = jnp.float32
BF16 = jnp.bfloat16
I32 = jnp.int32
HIGHEST = lax.Precision.HIGHEST

D_MODEL = 1024
SB_HEAD_DIM = 64
ML_HEADS = 4
ML_HEAD_DIM = 256
CONV_WIDTH = 4
N_EXPERTS = 32
TOP_K = 4
D_FF = 1024
SWIGLU_LIMIT = 7.0
SWIGLU_ALPHA = 1.702
N_MOD = 6
EPS = 1e-6
LOG2E = 1.4426950408889634

LANES = 128
SUBLANES = 8
DMA_UNROLL = 8

QUERY_TILE = 128
SB_Q_SPAN = 512
SB_K_SPAN = 512
ML_CHUNK = 256
ML_GROUP = 2
ROW_TILE_PROJ = 1024
ROW_TILE_MERGE = 512
ROW_TILE_MOE = 256
MOE_TOKEN_TILE = 512
VMEM_LIMIT = 56 * 1024 * 1024


def _nt(a, b, precision=None):
    return lax.dot_general(a, b, (((1,), (1,)), ((), ())), preferred_element_type=F32, precision=precision)


def _mm(a, b, precision=None):
    return jnp.dot(a, b, preferred_element_type=F32, precision=precision)


def _sigmoid(x):
    return 1.0 / (1.0 + jnp.exp(-x))


def _log_sigmoid(x):
    return jnp.minimum(x, 0.0) - jnp.log(1.0 + jnp.exp(-jnp.abs(x)))


def _ada_kernel(c_ref, w_ref, b_ref, o_ref):
    c = c_ref[...]
    o_ref[...] = _mm(c * _sigmoid(c), w_ref[...], precision=HIGHEST) + b_ref[...]


def _ada(c, ada_w, ada_b):
    bsz = c.shape[0]
    n = ada_w.shape[1]
    return pl.pallas_call(
        _ada_kernel,
        grid=(n // D_MODEL,),
        in_specs=[
            pl.BlockSpec((bsz, D_MODEL), lambda j: (0, 0)),
            pl.BlockSpec((D_MODEL, D_MODEL), lambda j: (0, j)),
            pl.BlockSpec((1, D_MODEL), lambda j: (0, j)),
        ],
        out_specs=pl.BlockSpec((bsz, D_MODEL), lambda j: (0, j)),
        out_shape=jax.ShapeDtypeStruct((bsz, n), F32),
        name="ada",
    )(c, ada_w, ada_b.reshape(1, n))


def _proj_kernel(x_ref, mod_ref, g_ref, w_ref, wg_ref, o_ref, og_ref, h_ref):
    @pl.when(pl.program_id(1) == 0)
    def _():
        x = x_ref[...]
        y = x * lax.rsqrt(jnp.mean(x * x, axis=-1, keepdims=True) + EPS) * g_ref[...]
        m = mod_ref[0]
        hb = (y * (1.0 + m[1:2]) + m[0:1]).astype(BF16)
        h_ref[...] = hb
        og_ref[...] = _mm(hb, wg_ref[...])

    o_ref[...] = _mm(h_ref[...], w_ref[...]).astype(BF16)


def _proj(x2, mod3, norm_g, w_main, w_gate, seq):
    n_tok = x2.shape[0]
    n_main = w_main.shape[1]
    tm = min(ROW_TILE_PROJ, seq)
    tn = D_MODEL
    return pl.pallas_call(
        _proj_kernel,
        grid=(n_tok // tm, n_main // tn),
        in_specs=[
            pl.BlockSpec((tm, D_MODEL), lambda i, j: (i, 0)),
            pl.BlockSpec((1, N_MOD, D_MODEL), lambda i, j: ((i * tm) // seq, 0, 0)),
            pl.BlockSpec((1, D_MODEL), lambda i, j: (0, 0)),
            pl.BlockSpec((D_MODEL, tn), lambda i, j: (0, j)),
            pl.BlockSpec((D_MODEL, LANES), lambda i, j: (0, 0)),
        ],
        out_specs=[
            pl.BlockSpec((tm, tn), lambda i, j: (i, j)),
            pl.BlockSpec((tm, LANES), lambda i, j: (i, 0)),
        ],
        out_shape=[
            jax.ShapeDtypeStruct((n_tok, n_main), BF16),
            jax.ShapeDtypeStruct((n_tok, LANES), F32),
        ],
        scratch_shapes=[pltpu.VMEM((tm, D_MODEL), BF16)],
        compiler_params=pltpu.CompilerParams(
            dimension_semantics=("arbitrary", "arbitrary"), vmem_limit_bytes=VMEM_LIMIT),
        name="proj",
    )(x2, mod3, norm_g.reshape(1, D_MODEL), w_main, w_gate)


def _sb_kernel(q_ref, k_ref, v_ref, uu_ref, o_ref, qh_s, run_s, acc_s):
    seq = q_ref.shape[0]
    qt = QUERY_TILE
    tq, tk = min(SB_Q_SPAN, seq), min(SB_K_SPAN, seq)
    n_heads = LANES // SB_HEAD_DIM
    n_qb, n_kb = tq // qt, tk // qt
    lane = lax.broadcasted_iota(I32, (qt, LANES), 1)
    strict = lane < lax.broadcasted_iota(I32, (qt, LANES), 0)
    uu = uu_ref[...]

    def group(s0, diag):
        k = k_ref[pl.ds(s0, tk), :]
        v = v_ref[pl.ds(s0, tk), :]
        chains = range(n_qb * n_heads)
        nbs = [ci // n_heads + 1 if diag else n_kb for ci in chains]
        blk = lambda x, b: x[:, b * qt:(b + 1) * qt]
        zs = [_nt(qh_s[ci], k[:nbs[ci] * qt]) for ci in chains]
        lkb = []
        for ci in chains:
            z = zs[ci]
            nz = -z
            lk = jnp.minimum(nz, 0.0) - jnp.log(1.0 + jnp.exp2(jnp.minimum(z, nz))) * LOG2E
            lks = [blk(lk, b) for b in range(nbs[ci])]
            if diag:
                lks[-1] = jnp.where(strict, lks[-1], 0.0)
            lkb.append([x.astype(BF16) for x in lks])
        ccs = [[_mm(lkb[ci][b], uu) for b in range(nbs[ci])] for ci in chains]
        probs = []
        for ci in chains:
            run = run_s[ci]
            a = [None] * nbs[ci]
            for b in reversed(range(nbs[ci])):
                a[b] = jnp.exp2(blk(zs[ci], b) + ccs[ci][b][:, :LANES] + run)
                run = run + ccs[ci][b][:, LANES:]
            run_s[ci] = run
            if diag:
                a[-1] = jnp.where(strict, a[-1], 0.0)
            probs.append(jnp.concatenate(a, axis=1).astype(BF16))
        for ci in chains:
            acc_s[ci] = acc_s[ci] + _mm(probs[ci], v[:nbs[ci] * qt])

    def qspan(i, carry):
        r0 = pl.multiple_of(i * tq, tq)
        run_s[...] = jnp.zeros_like(run_s)
        acc_s[...] = jnp.zeros_like(acc_s)
        for qb in range(n_qb):
            qf = q_ref[pl.ds(r0 + qb * qt, qt), :].astype(F32) * (SB_HEAD_DIM ** -0.5 * LOG2E)
            for h in range(n_heads):
                in_head = (lane >= h * SB_HEAD_DIM) & (lane < (h + 1) * SB_HEAD_DIM)
                qh_s[qb * n_heads + h] = jnp.where(in_head, qf, 0.0).astype(BF16)

        group(r0, True)

        def past(j, c):
            group(pl.multiple_of((i - 1 - j) * tk, tk), False)
            return c

        lax.fori_loop(0, i, past, 0)
        for qb in range(n_qb):
            o_ref[pl.ds(r0 + qb * qt, qt), :] = jnp.where(
                lane < SB_HEAD_DIM, acc_s[qb * n_heads], acc_s[qb * n_heads + 1]).astype(BF16)
        return carry

    lax.fori_loop(0, seq // tq, qspan, 0)


def _sb_attention(proj, uu, bsz, seq):
    n_pairs = D_MODEL // LANES
    tq, tk = min(SB_Q_SPAN, seq), min(SB_K_SPAN, seq)
    assert tk == tq and seq % tk == 0 and LANES // SB_HEAD_DIM == 2
    n_chain = (tq // QUERY_TILE) * (LANES // SB_HEAD_DIM)
    return pl.pallas_call(
        _sb_kernel,
        grid=(bsz, n_pairs),
        in_specs=[
            pl.BlockSpec((seq, LANES), lambda b, p: (b, p)),
            pl.BlockSpec((seq, LANES), lambda b, p: (b, n_pairs + p)),
            pl.BlockSpec((seq, LANES), lambda b, p: (b, 2 * n_pairs + p)),
            pl.BlockSpec((LANES, 2 * LANES), lambda b, p: (0, 0)),
        ],
        out_specs=pl.BlockSpec((seq, LANES), lambda b, p: (b, p)),
        out_shape=jax.ShapeDtypeStruct((bsz * seq, D_MODEL), BF16),
        scratch_shapes=[
            pltpu.VMEM((n_chain, QUERY_TILE, LANES), BF16),
            pltpu.VMEM((n_chain, QUERY_TILE, LANES), F32),
            pltpu.VMEM((n_chain, QUERY_TILE, LANES), F32),
        ],
        compiler_params=pltpu.CompilerParams(dimension_semantics=("arbitrary", "arbitrary")),
        name="sb_attn",
    )(proj, proj, proj, uu)


def _mlstm_kernel(q_ref, k_ref, v_ref, og_ref, gt_ref, gb_ref, cwq_ref, cwk_ref, cbq_ref, cbk_ref, ng_ref, tri_ref,
                  y_ref, xq_s, xk_s, c_s, n_s, m_s):
    hd = ML_HEAD_DIM
    heads = range(ML_GROUP)
    first = pl.program_id(1) * ML_GROUP
    seq = q_ref.shape[0]
    cl = min(ML_CHUNK, seq)
    pad = SUBLANES
    for h in heads:
        xq_s[h, 0:pad, :] = jnp.zeros((pad, hd), F32)
        xk_s[h, 0:pad, :] = jnp.zeros((pad, hd), F32)
    c_s[...] = jnp.zeros_like(c_s)
    n_s[...] = jnp.zeros_like(n_s)
    m_s[...] = jnp.zeros_like(m_s)
    lane = lax.broadcasted_iota(I32, (cl, LANES), 1)
    sel_i = [(lane == first + h).astype(F32) for h in heads]
    sel_f = [(lane == first + h + ML_HEADS).astype(F32) for h in heads]
    tril = lax.broadcasted_iota(I32, (cl, cl), 1) <= lax.broadcasted_iota(I32, (cl, cl), 0)
    cols = lambda h: slice(h * hd, (h + 1) * hd)

    def conv_silu(x_s, h, cw_ref, cb_ref):
        y = cb_ref[:, cols(h)]
        for tap in range(CONV_WIDTH):
            lo = pad - (CONV_WIDTH - 1) + tap
            y = y + x_s[h, lo:lo + cl, :] * cw_ref[tap:tap + 1, cols(h)]
        x_s[h, 0:pad, :] = x_s[h, cl:cl + pad, :]
        return y * _sigmoid(y)

    def chunk(c, carry):
        rows = pl.ds(pl.multiple_of(c * cl, cl), cl)
        for h in heads:
            xq_s[h, pad:pad + cl, :] = q_ref[rows, cols(h)].astype(F32)
            xk_s[h, pad:pad + cl, :] = k_ref[rows, cols(h)].astype(F32)
        qc = [conv_silu(xq_s, h, cwq_ref, cbq_ref) for h in heads]
        kc = [conv_silu(xk_s, h, cwk_ref, cbk_ref) * (hd ** -0.5) for h in heads]
        qb = [x.astype(BF16) for x in qc]
        kb = [x.astype(BF16) for x in kc]
        vb = [v_ref[rows, cols(h)] for h in heads]

        pre = gt_ref[rows, :] + gb_ref[...]
        bt = _mm(tri_ref[...], _log_sigmoid(pre), precision=HIGHEST)
        b_col = [jnp.sum(bt * sel_f[h], axis=1, keepdims=True) for h in heads]
        i_col = [jnp.sum(pre * sel_i[h], axis=1, keepdims=True) for h in heads]
        g_col = [i_col[h] - b_col[h] for h in heads]
        g_row = [jnp.transpose(jnp.broadcast_to(g_col[h], (cl, LANES)))[0:1, :] for h in heads]

        m_prev = [m_s[h] for h in heads]
        dmat = [jnp.where(tril, b_col[h] + g_row[h], -jnp.inf) for h in heads]
        m_row = [jnp.maximum(b_col[h] + m_prev[h], jnp.max(dmat[h], axis=1, keepdims=True)) for h in heads]
        w_inter = [jnp.exp(b_col[h] + m_prev[h] - m_row[h]) for h in heads]
        qk = [_nt(qb[h], kb[h]) for h in heads]
        s_mat = [qk[h] * jnp.exp(dmat[h] - m_row[h]) for h in heads]
        q_c = [_mm(qb[h], c_s[h].astype(BF16)) for h in heads]
        s_v = [_mm(s_mat[h].astype(BF16), vb[h]) for h in heads]
        den = [w_inter[h] * jnp.sum(qc[h] * n_s[h], axis=1, keepdims=True) + jnp.sum(s_mat[h], axis=1, keepdims=True)
               for h in heads]
        hh = [(w_inter[h] * q_c[h] + s_v[h]) * (1.0 / jnp.maximum(jnp.abs(den[h]), jnp.exp(-m_row[h])))
              for h in heads]

        m_new = [m_row[h][cl - 1:cl, :] for h in heads]
        b_last = [b_col[h][cl - 1:cl, :] for h in heads]
        decay = [jnp.exp(b_last[h] + m_prev[h] - m_new[h]) for h in heads]
        wk = [jnp.exp(b_last[h] + g_col[h] - m_new[h]) * kc[h] for h in heads]
        k_v = [lax.dot_general(wk[h].astype(BF16), vb[h], (((0,), (0,)), ((), ())), preferred_element_type=F32)
               for h in heads]
        for h in heads:
            c_s[h] = decay[h] * c_s[h] + k_v[h]
            n_s[h] = decay[h] * n_s[h] + jnp.sum(wk[h], axis=0, keepdims=True)
            m_s[h] = m_new[h]
        for h in heads:
            hn = hh[h] * lax.rsqrt(jnp.mean(hh[h] * hh[h], axis=1, keepdims=True) + EPS) * ng_ref[:, cols(h)]
            y_ref[rows, cols(h)] = (hn * _sigmoid(og_ref[rows, cols(h)].astype(F32))).astype(BF16)
        return carry

    lax.fori_loop(0, seq // cl, chunk, 0)


def _mlstm(proj, gates, gate_bias, conv_w, conv_b, norm_g, tri, bsz, seq):
    hd = ML_HEAD_DIM
    g = ML_GROUP
    cl = min(ML_CHUNK, seq)
    n_grp = ML_HEADS // g
    col0 = 3 * D_MODEL // (g * hd)
    row_spec = lambda off: pl.BlockSpec((seq, g * hd), lambda b, j: (b, off + j))
    return pl.pallas_call(
        _mlstm_kernel,
        grid=(bsz, n_grp),
        in_specs=[
            row_spec(col0), row_spec(col0 + n_grp), row_spec(col0 + 2 * n_grp), row_spec(col0 + 3 * n_grp),
            pl.BlockSpec((seq, LANES), lambda b, j: (b, 0)),
            pl.BlockSpec((1, LANES), lambda b, j: (0, 0)),
            pl.BlockSpec((CONV_WIDTH, g * hd), lambda b, j: (0, j)),
            pl.BlockSpec((CONV_WIDTH, g * hd), lambda b, j: (0, n_grp + j)),
            pl.BlockSpec((1, g * hd), lambda b, j: (0, j)),
            pl.BlockSpec((1, g * hd), lambda b, j: (0, n_grp + j)),
            pl.BlockSpec((1, g * hd), lambda b, j: (0, j)),
            pl.BlockSpec((cl, cl), lambda b, j: (0, 0)),
        ],
        out_specs=pl.BlockSpec((seq, g * hd), lambda b, j: (b, j)),
        out_shape=jax.ShapeDtypeStruct((bsz * seq, D_MODEL), BF16),
        scratch_shapes=[
            pltpu.VMEM((g, cl + 2 * SUBLANES, hd), F32),
            pltpu.VMEM((g, cl + 2 * SUBLANES, hd), F32),
            pltpu.VMEM((g, hd, hd), F32),
            pltpu.VMEM((g, 1, hd), F32),
            pltpu.VMEM((g, 1, 1), F32),
        ],
        compiler_params=pltpu.CompilerParams(
            dimension_semantics=("arbitrary", "arbitrary"), vmem_limit_bytes=VMEM_LIMIT),
        name="mlstm",
    )(proj, proj, proj, proj, gates, gate_bias, conv_w, conv_w,
      conv_b.reshape(1, -1), conv_b.reshape(1, -1), norm_g.reshape(1, -1), tri)


def _merge_kernel(ya_ref, yb_ref, ga_ref, gb_ref, x_ref, mod_ref, wa_ref, wb_ref, wo_ref, n2_ref, rwt_ref, rb_ref,
                  su_ref, x1_ref, h2_ref, idx_ref, wt_ref, rank_ref, cnt_ref, carry_s):
    @pl.when(pl.program_id(0) == 0)
    def _():
        carry_s[...] = jnp.zeros_like(carry_s)

    tm = x_ref.shape[0]
    a = _mm(ya_ref[...], wa_ref[...])
    b = _mm(yb_ref[...], wb_ref[...])
    merged = _sigmoid(ga_ref[...].astype(F32)) * a + _sigmoid(gb_ref[...].astype(F32)) * b
    m = mod_ref[0]
    x1 = x_ref[...] + m[2:3] * _mm(merged.astype(BF16), wo_ref[...])
    x1_ref[...] = x1
    y = x1 * lax.rsqrt(jnp.mean(x1 * x1, axis=-1, keepdims=True) + EPS) * n2_ref[...]
    h2 = y * (1.0 + m[4:5]) + m[3:4]
    h2_ref[...] = h2

    logits = _nt(rwt_ref[...], h2, precision=HIGHEST) + rb_ref[...]
    ie = lax.broadcasted_iota(I32, (N_EXPERTS, tm), 0).astype(F32)
    idxs, vals = [], []
    for _ in range(TOP_K):
        mx = jnp.max(logits, axis=0, keepdims=True)
        am = jnp.min(jnp.where(logits == mx, ie, float(N_EXPERTS)), axis=0, keepdims=True)
        idxs.append(am)
        vals.append(mx)
        logits = jnp.where(ie == am, -jnp.inf, logits)
    exps = [jnp.exp(v - vals[0]) for v in vals]
    inv = 1.0 / (exps[0] + exps[1] + exps[2] + exps[3])
    wt_ref[...] = jnp.concatenate([e * inv for e in exps], axis=0)
    idx_ref[...] = jnp.concatenate(idxs, axis=0).astype(I32)

    onehot = jnp.zeros((N_EXPERTS, tm), F32)
    for am in idxs:
        onehot = onehot + (ie == am).astype(F32)
    before = _mm(onehot.astype(BF16), su_ref[...]) + carry_s[:, 0:1]
    ranks = [jnp.sum(jnp.where(ie == am, before, 0.0), axis=0, keepdims=True) for am in idxs]
    rank_ref[...] = jnp.concatenate(ranks, axis=0).astype(I32)
    carry_s[...] = carry_s[...] + jnp.sum(onehot, axis=1, keepdims=True)
    cnt_ref[...] = carry_s[...]


def _merge(ya, yb, proj, x2, mod3, w_a, w_b, w_o, norm2_g, router_wt, router_b, su, seq):
    n_tok = x2.shape[0]
    tm = min(ROW_TILE_MERGE, seq)
    d = D_MODEL
    gcol = 7 * d // d
    row = lambda j: pl.BlockSpec((tm, d), lambda i: (i, j))
    const = lambda shape: pl.BlockSpec(shape, lambda i: tuple(0 for _ in shape))
    tok4 = pl.BlockSpec((TOP_K, tm), lambda i: (0, i))
    return pl.pallas_call(
        _merge_kernel,
        grid=(n_tok // tm,),
        in_specs=[
            row(0), row(0), row(gcol), row(gcol + 1), row(0),
            pl.BlockSpec((1, N_MOD, d), lambda i: ((i * tm) // seq, 0, 0)),
            const((d, d)), const((d, d)), const((d, d)), const((1, d)),
            const((N_EXPERTS, d)), const((N_EXPERTS, 1)), const((tm, tm)),
        ],
        out_specs=[
            row(0), row(0),
            tok4, tok4, tok4,
            const((N_EXPERTS, LANES)),
        ],
        out_shape=[
            jax.ShapeDtypeStruct((n_tok, d), F32),
            jax.ShapeDtypeStruct((n_tok, d), F32),
            jax.ShapeDtypeStruct((TOP_K, n_tok), I32),
            jax.ShapeDtypeStruct((TOP_K, n_tok), F32),
            jax.ShapeDtypeStruct((TOP_K, n_tok), I32),
            jax.ShapeDtypeStruct((N_EXPERTS, LANES), F32),
        ],
        scratch_shapes=[pltpu.VMEM((N_EXPERTS, LANES), F32)],
        compiler_params=pltpu.CompilerParams(dimension_semantics=("arbitrary",), vmem_limit_bytes=VMEM_LIMIT),
        name="merge",
    )(ya, yb, proj, proj, x2, mod3, w_a, w_b, w_o, norm2_g.reshape(1, d), router_wt, router_b.reshape(-1, 1), su)


def _scatter_kernel(fill_lo_ref, fill_hi_ref, nb_ref, dest_ref, h2_ref, xs_ref, zero_s, sem, zsem):
    tm = h2_ref.shape[0]
    bm = zero_s.shape[0]

    @pl.when(pl.program_id(0) == 0)
    def _():
        zero_s[...] = jnp.zeros_like(zero_s)

        def per_expert(e, total):
            lo, hi = fill_lo_ref[e], fill_hi_ref[e]

            def fill(r, c):
                pltpu.make_async_copy(zero_s.at[pl.ds(0, 1), :], xs_ref.at[pl.ds(r, 1), :], sem).start()
                return c

            lax.fori_loop(lo, hi, fill, 0)
            return total + (hi - lo)

        total = lax.fori_loop(0, N_EXPERTS, per_expert, 0)

        def drain(r, c):
            pltpu.make_async_copy(zero_s.at[pl.ds(0, 1), :], xs_ref.at[pl.ds(0, 1), :], sem).wait()
            return c

        lax.fori_loop(0, total, drain, 0)

        n_blocks = xs_ref.shape[0] // bm

        def fill_block(j, c):
            pltpu.make_async_copy(zero_s, xs_ref.at[pl.ds(pl.multiple_of(j * bm, bm), bm), :], zsem).start()
            return c

        lax.fori_loop(nb_ref[0], n_blocks, fill_block, 0)

        def drain_block(j, c):
            pltpu.make_async_copy(zero_s, xs_ref.at[pl.ds(0, bm), :], zsem).wait()
            return c

        lax.fori_loop(nb_ref[0], n_blocks, drain_block, 0)

    def send(g, c):
        for u in range(DMA_UNROLL):
            t = g * DMA_UNROLL + u
            for r in range(TOP_K):
                pltpu.make_async_copy(
                    h2_ref.at[pl.ds(t, 1), :], xs_ref.at[pl.ds(dest_ref[r, t], 1), :], sem).start()
        return c

    lax.fori_loop(0, tm // DMA_UNROLL, send, 0)

    def drain(g, c):
        for _ in range(DMA_UNROLL * TOP_K):
            pltpu.make_async_copy(h2_ref.at[pl.ds(0, 1), :], xs_ref.at[pl.ds(0, 1), :], sem).wait()
        return c

    lax.fori_loop(0, tm // DMA_UNROLL, drain, 0)


def _scatter(fill_lo, fill_hi, n_used, dest, h2s, n_rows, seq):
    n_tok = h2s.shape[0]
    tm = min(MOE_TOKEN_TILE, seq)
    smem = lambda: pl.BlockSpec(memory_space=pltpu.SMEM)
    tok4 = pl.BlockSpec((TOP_K, tm), lambda i: (0, i), memory_space=pltpu.SMEM)
    return pl.pallas_call(
        _scatter_kernel,
        grid=(n_tok // tm,),
        in_specs=[smem(), smem(), smem(), tok4, pl.BlockSpec((tm, D_MODEL), lambda i: (i, 0))],
        out_specs=pl.BlockSpec(memory_space=pl.ANY),
        out_shape=jax.ShapeDtypeStruct((n_rows, D_MODEL), F32),
        scratch_shapes=[pltpu.VMEM((ROW_TILE_MOE, D_MODEL), F32), pltpu.SemaphoreType.DMA,
                        pltpu.SemaphoreType.DMA],
        compiler_params=pltpu.CompilerParams(dimension_semantics=("arbitrary",)),
        name="scatter",
    )(fill_lo, fill_hi, n_used, dest, h2s)


def _expert_kernel(be_ref, nb_ref, xs_ref, w1_ref, b1_ref, w2_ref, b2_ref, ys_ref, w1_s, w2_s):
    j = pl.program_id(0)
    changed = jnp.logical_or(j == 0, be_ref[j] != be_ref[jnp.maximum(j - 1, 0)])

    @pl.when(changed)
    def _():
        w1_s[...] = w1_ref[...].astype(BF16)
        w2_s[...] = w2_ref[...].astype(BF16)

    @pl.when(j < nb_ref[0])
    def _():
        a = _mm(xs_ref[...].astype(BF16), w1_s[...]) + b1_ref[...]
        glu = jnp.minimum(a[:, :D_FF], SWIGLU_LIMIT)
        lin = jnp.clip(a[:, D_FF:], -SWIGLU_LIMIT, SWIGLU_LIMIT)
        hmid = (lin + 1.0) * (glu * _sigmoid(SWIGLU_ALPHA * glu))
        ys_ref[...] = _mm(hmid.astype(BF16), w2_s[...]) + b2_ref[...]

    @pl.when(j >= nb_ref[0])
    def _():
        ys_ref[...] = jnp.zeros_like(ys_ref)


def _experts(block_e, n_blocks_used, xs, w1, b1, w2, b2):
    n_rows = xs.shape[0]
    bm = ROW_TILE_MOE
    d = D_MODEL
    blk = lambda j, be, nb: (j, 0)
    exp = lambda j, be, nb: (be[j], 0, 0)
    return pl.pallas_call(
        _expert_kernel,
        grid_spec=pltpu.PrefetchScalarGridSpec(
            num_scalar_prefetch=2,
            grid=(n_rows // bm,),
            in_specs=[
                pl.BlockSpec((bm, d), blk),
                pl.BlockSpec((None, d, 2 * D_FF), exp),
                pl.BlockSpec((None, 1, 2 * D_FF), exp),
                pl.BlockSpec((None, D_FF, d), exp),
                pl.BlockSpec((None, 1, d), exp),
            ],
            out_specs=pl.BlockSpec((bm, d), blk),
            scratch_shapes=[pltpu.VMEM((d, 2 * D_FF), BF16), pltpu.VMEM((D_FF, d), BF16)],
        ),
        out_shape=jax.ShapeDtypeStruct((n_rows, d), F32),
        compiler_params=pltpu.CompilerParams(dimension_semantics=("arbitrary",), vmem_limit_bytes=VMEM_LIMIT),
        name="experts",
    )(block_e, n_blocks_used, xs, w1, b1.reshape(N_EXPERTS, 1, -1), w2, b2.reshape(N_EXPERTS, 1, -1))


def _combine_kernel(dest_ref, wt_ref, ys_ref, x1_ref, mod_ref, fg_ref, o_ref, buf_s, sem):
    tm = x1_ref.shape[0]

    def fetch(g, c):
        for u in range(DMA_UNROLL):
            t = g * DMA_UNROLL + u
            for r in range(TOP_K):
                pltpu.make_async_copy(
                    ys_ref.at[pl.ds(dest_ref[r, t], 1), :], buf_s.at[r, pl.ds(t, 1), :], sem).start()
        return c

    lax.fori_loop(0, tm // DMA_UNROLL, fetch, 0)

    def drain(g, c):
        for _ in range(DMA_UNROLL * TOP_K):
            pltpu.make_async_copy(ys_ref.at[pl.ds(0, 1), :], buf_s.at[0, pl.ds(0, 1), :], sem).wait()
        return c

    lax.fori_loop(0, tm // DMA_UNROLL, drain, 0)

    wt = wt_ref[...]
    moe = wt[:, 0:1] * buf_s[0]
    for r in range(1, TOP_K):
        moe = moe + wt[:, r:r + 1] * buf_s[r]
    x2 = x1_ref[...] + mod_ref[0][5:6] * moe
    o_ref[...] = x2 * lax.rsqrt(jnp.mean(x2 * x2, axis=-1, keepdims=True) + EPS) * fg_ref[...]


def _combine(dest, wt, ys, x1, mod3, final_g, seq):
    n_tok = x1.shape[0]
    tm = min(MOE_TOKEN_TILE, seq)
    d = D_MODEL
    tok4 = pl.BlockSpec((TOP_K, tm), lambda i: (0, i), memory_space=pltpu.SMEM)
    return pl.pallas_call(
        _combine_kernel,
        grid=(n_tok // tm,),
        in_specs=[
            tok4,
            pl.BlockSpec((tm, TOP_K), lambda i: (i, 0)),
            pl.BlockSpec(memory_space=pl.ANY),
            pl.BlockSpec((tm, d), lambda i: (i, 0)),
            pl.BlockSpec((1, N_MOD, d), lambda i: ((i * tm) // seq, 0, 0)),
            pl.BlockSpec((1, d), lambda i: (0, 0)),
        ],
        out_specs=pl.BlockSpec((tm, d), lambda i: (i, 0)),
        out_shape=jax.ShapeDtypeStruct((n_tok, d), F32),
        scratch_shapes=[pltpu.VMEM((TOP_K, tm, d), F32), pltpu.SemaphoreType.DMA],
        compiler_params=pltpu.CompilerParams(dimension_semantics=("arbitrary",), vmem_limit_bytes=VMEM_LIMIT),
        name="combine",
    )(dest, wt.T, ys, x1, mod3, final_g.reshape(1, d))


def _layer(x2, mod3, bsz, seq, norm1_g, w_in, conv_w, conv_b, ml_b_i, ml_b_f, ml_norm_g, w_branch_a, w_branch_b,
           w_out, norm2_g, router_w, router_b, expert_w1, expert_b1, expert_w2, expert_b2):
    d = D_MODEL
    n_tok = bsz * seq
    gate0 = 7 * d
    n_gate = 2 * ML_HEADS
    w_main = jnp.concatenate([w_in[:, :gate0], w_in[:, gate0 + n_gate:]], axis=1).astype(BF16)
    w_gate = jnp.pad(w_in[:, gate0:gate0 + n_gate], ((0, 0), (0, LANES - n_gate))).astype(BF16)
    gate_bias = jnp.pad(jnp.concatenate([ml_b_i, ml_b_f]), (0, LANES - n_gate)).reshape(1, LANES)

    proj, gates = _proj(x2, mod3, norm1_g, w_main, w_gate, seq)

    ones = jnp.ones((LANES, LANES), F32)
    uu = jnp.concatenate([jnp.tril(ones), ones], axis=1).astype(BF16)
    ya = _sb_attention(proj, uu, bsz, seq)

    cl = min(ML_CHUNK, seq)
    tri = jnp.tril(jnp.ones((cl, cl), F32))
    yb = _mlstm(proj, gates, gate_bias, conv_w, conv_b, ml_norm_g, tri, bsz, seq)

    tm = min(ROW_TILE_MERGE, seq)
    su = jnp.triu(jnp.ones((tm, tm), F32), k=1).astype(BF16)
    x1, h2s, idx, wt, rank, cnt = _merge(
        ya, yb, proj, x2, mod3, w_branch_a.astype(BF16), w_branch_b.astype(BF16), w_out.astype(BF16),
        norm2_g, router_w.T, router_b, su, seq)

    bm = ROW_TILE_MOE
    counts = cnt[:, 0].astype(I32)
    padded = (counts + bm - 1) // bm * bm
    pad_end = jnp.cumsum(padded)
    start = pad_end - padded
    n_rows = (n_tok * TOP_K + N_EXPERTS * (bm - 1)) // bm * bm
    n_blocks = n_rows // bm
    n_used = jnp.maximum(pad_end[-1] // bm, 1).astype(I32)
    blk = jnp.minimum(jnp.arange(n_blocks, dtype=I32), n_used - 1)
    block_e = jnp.minimum(jnp.sum((pad_end[None, :] <= (blk * bm)[:, None]).astype(I32), axis=1), N_EXPERTS - 1)

    onehot = idx[:, :, None] == jnp.arange(N_EXPERTS, dtype=I32)
    dest = rank + jnp.sum(jnp.where(onehot, start, 0), axis=-1)

    xs = _scatter(start + counts, pad_end, n_used.reshape(1), dest, h2s, n_rows, seq)
    ys = _experts(block_e, n_used.reshape(1), xs, expert_w1, expert_b1, expert_w2, expert_b2)
    return dest, wt, ys, x1


def kernel(x, c, ada_w, ada_b, norm1_g, w_in, conv_w, conv_b, ml_b_i, ml_b_f, ml_norm_g, w_branch_a, w_branch_b, w_out, norm2_g, router_w, router_b, expert_w1, expert_b1, expert_w2, expert_b2, final_g):
    bsz, seq, d = x.shape
    depth = ada_w.shape[0]
    assert d == D_MODEL and depth == 1 and seq % QUERY_TILE == 0
    x2 = x.reshape(bsz * seq, d)
    mod3 = _ada(c, ada_w[0], ada_b[0]).reshape(bsz, N_MOD, d)
    dest, wt, ys, x1 = _layer(
        x2, mod3, bsz, seq, norm1_g[0], w_in[0], conv_w[0], conv_b[0], ml_b_i[0], ml_b_f[0], ml_norm_g[0],
        w_branch_a[0], w_branch_b[0], w_out[0], norm2_g[0], router_w[0], router_b[0],
        expert_w1[0], expert_b1[0], expert_w2[0], expert_b2[0])
    out = _combine(dest, wt, ys, x1, mod3, final_g, seq)
    return out.reshape(bsz, seq, d)
```

```python
import functools

import jax
import jax.numpy as jnp
from jax import lax
from jax.experimental import pallas as pl
from jax.experimental.pallas import tpu as pltpu

F32 = jnp.float32
BF16 = jnp.bfloat16
I32 = jnp.int32
HIGHEST = lax.Precision.HIGHEST

D_MODEL = 1024
SB_HEAD_DIM = 64
ML_HEADS = 4
ML_HEAD_DIM = 256
CONV_WIDTH = 4
N_EXPERTS = 32
TOP_K = 4
D_FF = 1024
SWIGLU_LIMIT = 7.0
SWIGLU_ALPHA = 1.702
N_MOD = 6
EPS = 1e-6
LOG2E = 1.4426950408889634

LANES = 128
SUBLANES = 8
DMA_UNROLL = 8

QUERY_TILE = 128
SB_Q_SPAN = 512
SB_K_SPAN = 512
ML_CHUNK = 256
ML_GROUP = 4
ROW_TILE_PROJ = 1024
ROW_TILE_MERGE = 512
MERGE_SUB = 1
ROW_TILE_MOE = 512
MOE_TOKEN_TILE = 512
VMEM_LIMIT = 56 * 1024 * 1024


def _nt(a, b, precision=None):
    return lax.dot_general(a, b, (((1,), (1,)), ((), ())), preferred_element_type=F32, precision=precision)


def _mm(a, b, precision=None):
    return jnp.dot(a, b, preferred_element_type=F32, precision=precision)


def _sigmoid(x):
    return 1.0 / (1.0 + jnp.exp(-x))


def _log_sigmoid(x):
    return jnp.minimum(x, 0.0) - jnp.log(1.0 + jnp.exp(-jnp.abs(x)))


def _ada_kernel(c_ref, w_ref, b_ref, o_ref):
    c = c_ref[...]
    o_ref[...] = _mm(c * _sigmoid(c), w_ref[...], precision=HIGHEST) + b_ref[...]


def _ada(c, ada_w, ada_b):
    bsz = c.shape[0]
    n = ada_w.shape[1]
    return pl.pallas_call(
        _ada_kernel,
        grid=(n // D_MODEL,),
        in_specs=[
            pl.BlockSpec((bsz, D_MODEL), lambda j: (0, 0)),
            pl.BlockSpec((D_MODEL, D_MODEL), lambda j: (0, j)),
            pl.BlockSpec((1, D_MODEL), lambda j: (0, j)),
        ],
        out_specs=pl.BlockSpec((bsz, D_MODEL), lambda j: (0, j)),
        out_shape=jax.ShapeDtypeStruct((bsz, n), F32),
        name="ada",
    )(c, ada_w, ada_b.reshape(1, n))


def _proj_kernel(x_ref, mod_ref, g_ref, w_ref, wg_ref, o_ref, og_ref, h_ref):
    @pl.when(pl.program_id(1) == 0)
    def _():
        x = x_ref[...]
        y = x * lax.rsqrt(jnp.mean(x * x, axis=-1, keepdims=True) + EPS) * g_ref[...]
        m = mod_ref[0]
        hb = (y * (1.0 + m[1:2]) + m[0:1]).astype(BF16)
        h_ref[...] = hb
        og_ref[...] = _mm(hb, wg_ref[...])

    o_ref[...] = _mm(h_ref[...], w_ref[...]).astype(BF16)


def _proj(x2, mod3, norm_g, w_main, w_gate, seq):
    n_tok = x2.shape[0]
    n_main = w_main.shape[1]
    tm = min(ROW_TILE_PROJ, seq)
    tn = D_MODEL
    return pl.pallas_call(
        _proj_kernel,
        grid=(n_tok // tm, n_main // tn),
        in_specs=[
            pl.BlockSpec((tm, D_MODEL), lambda i, j: (i, 0)),
            pl.BlockSpec((1, N_MOD, D_MODEL), lambda i, j: ((i * tm) // seq, 0, 0)),
            pl.BlockSpec((1, D_MODEL), lambda i, j: (0, 0)),
            pl.BlockSpec((D_MODEL, tn), lambda i, j: (0, j)),
            pl.BlockSpec((D_MODEL, LANES), lambda i, j: (0, 0)),
        ],
        out_specs=[
            pl.BlockSpec((tm, tn), lambda i, j: (i, j)),
            pl.BlockSpec((tm, LANES), lambda i, j: (i, 0)),
        ],
        out_shape=[
            jax.ShapeDtypeStruct((n_tok, n_main), BF16),
            jax.ShapeDtypeStruct((n_tok, LANES), F32),
        ],
        scratch_shapes=[pltpu.VMEM((tm, D_MODEL), BF16)],
        compiler_params=pltpu.CompilerParams(
            dimension_semantics=("arbitrary", "arbitrary"), vmem_limit_bytes=VMEM_LIMIT),
        name="proj",
    )(x2, mod3, norm_g.reshape(1, D_MODEL), w_main, w_gate)


def _sb_kernel(q_ref, k_ref, v_ref, uu_ref, o_ref, qh_s, run_s, acc_s):
    seq = q_ref.shape[0]
    qt = QUERY_TILE
    tq, tk = min(SB_Q_SPAN, seq), min(SB_K_SPAN, seq)
    n_heads = LANES // SB_HEAD_DIM
    n_qb, n_kb = tq // qt, tk // qt
    lane = lax.broadcasted_iota(I32, (qt, LANES), 1)
    strict = lane < lax.broadcasted_iota(I32, (qt, LANES), 0)
    uu = uu_ref[...]

    def group(s0, diag):
        k = k_ref[pl.ds(s0, tk), :]
        v = v_ref[pl.ds(s0, tk), :]
        chains = range(n_qb * n_heads)
        nbs = [ci // n_heads + 1 if diag else n_kb for ci in chains]
        blk = lambda x, b: x[:, b * qt:(b + 1) * qt]
        zs = [_nt(qh_s[ci], k[:nbs[ci] * qt]) for ci in chains]
        lkb = []
        for ci in chains:
            z = zs[ci]
            nz = -z
            lk = jnp.minimum(nz, 0.0) - jnp.log(1.0 + jnp.exp2(jnp.minimum(z, nz))) * LOG2E
            lks = [blk(lk, b) for b in range(nbs[ci])]
            if diag:
                lks[-1] = jnp.where(strict, lks[-1], 0.0)
            lkb.append([x.astype(BF16) for x in lks])
        ccs = [[_mm(lkb[ci][b], uu) for b in range(nbs[ci])] for ci in chains]
        probs = []
        for ci in chains:
            run = run_s[ci]
            a = [None] * nbs[ci]
            for b in reversed(range(nbs[ci])):
                a[b] = jnp.exp2(blk(zs[ci], b) + ccs[ci][b][:, :LANES] + run)
                run = run + ccs[ci][b][:, LANES:]
            run_s[ci] = run
            if diag:
                a[-1] = jnp.where(strict, a[-1], 0.0)
            probs.append(jnp.concatenate(a, axis=1).astype(BF16))
        for ci in chains:
            acc_s[ci] = acc_s[ci] + _mm(probs[ci], v[:nbs[ci] * qt])

    def qspan(i, carry):
        r0 = pl.multiple_of(i * tq, tq)
        run_s[...] = jnp.zeros_like(run_s)
        acc_s[...] = jnp.zeros_like(acc_s)
        for qb in range(n_qb):
            qf = q_ref[pl.ds(r0 + qb * qt, qt), :].astype(F32) * (SB_HEAD_DIM ** -0.5 * LOG2E)
            for h in range(n_heads):
                in_head = (lane >= h * SB_HEAD_DIM) & (lane < (h + 1) * SB_HEAD_DIM)
                qh_s[qb * n_heads + h] = jnp.where(in_head, qf, 0.0).astype(BF16)

        group(r0, True)

        def past(j, c):
            group(pl.multiple_of((i - 1 - j) * tk, tk), False)
            return c

        lax.fori_loop(0, i, past, 0)
        for qb in range(n_qb):
            o_ref[pl.ds(r0 + qb * qt, qt), :] = jnp.where(
                lane < SB_HEAD_DIM, acc_s[qb * n_heads], acc_s[qb * n_heads + 1]).astype(BF16)
        return carry

    lax.fori_loop(0, seq // tq, qspan, 0)


def _sb_attention(proj, uu, bsz, seq):
    n_pairs = D_MODEL // LANES
    tq, tk = min(SB_Q_SPAN, seq), min(SB_K_SPAN, seq)
    assert tk == tq and seq % tk == 0 and LANES // SB_HEAD_DIM == 2
    n_chain = (tq // QUERY_TILE) * (LANES // SB_HEAD_DIM)
    return pl.pallas_call(
        _sb_kernel,
        grid=(bsz, n_pairs),
        in_specs=[
            pl.BlockSpec((seq, LANES), lambda b, p: (b, p)),
            pl.BlockSpec((seq, LANES), lambda b, p: (b, n_pairs + p)),
            pl.BlockSpec((seq, LANES), lambda b, p: (b, 2 * n_pairs + p)),
            pl.BlockSpec((LANES, 2 * LANES), lambda b, p: (0, 0)),
        ],
        out_specs=pl.BlockSpec((seq, LANES), lambda b, p: (b, p)),
        out_shape=jax.ShapeDtypeStruct((bsz * seq, D_MODEL), BF16),
        scratch_shapes=[
            pltpu.VMEM((n_chain, QUERY_TILE, LANES), BF16),
            pltpu.VMEM((n_chain, QUERY_TILE, LANES), F32),
            pltpu.VMEM((n_chain, QUERY_TILE, LANES), F32),
        ],
        compiler_params=pltpu.CompilerParams(dimension_semantics=("arbitrary", "arbitrary")),
        name="sb_attn",
    )(proj, proj, proj, uu)


def _mlstm_kernel(q_ref, k_ref, v_ref, og_ref, gt_ref, gb_ref, cwq_ref, cwk_ref, cbq_ref, cbk_ref, ng_ref, tri_ref,
                  y_ref, xq_s, xk_s, c_s, n_s, m_s):
    hd = ML_HEAD_DIM
    heads = range(ML_GROUP)
    first = pl.program_id(1) * ML_GROUP
    seq = q_ref.shape[0]
    cl = min(ML_CHUNK, seq)
    pad = SUBLANES
    for h in heads:
        xq_s[h, 0:pad, :] = jnp.zeros((pad, hd), F32)
        xk_s[h, 0:pad, :] = jnp.zeros((pad, hd), F32)
    c_s[...] = jnp.zeros_like(c_s)
    n_s[...] = jnp.zeros_like(n_s)
    m_s[...] = jnp.zeros_like(m_s)
    lane = lax.broadcasted_iota(I32, (cl, LANES), 1)
    sel_i = [(lane == first + h).astype(F32) for h in heads]
    sel_f = [(lane == first + h + ML_HEADS).astype(F32) for h in heads]
    tril = lax.broadcasted_iota(I32, (cl, cl), 1) <= lax.broadcasted_iota(I32, (cl, cl), 0)
    cols = lambda h: slice(h * hd, (h + 1) * hd)

    def conv_silu(x_s, h, cw_ref, cb_ref):
        y = cb_ref[:, cols(h)]
        for tap in range(CONV_WIDTH):
            lo = pad - (CONV_WIDTH - 1) + tap
            y = y + x_s[h, lo:lo + cl, :] * cw_ref[tap:tap + 1, cols(h)]
        x_s[h, 0:pad, :] = x_s[h, cl:cl + pad, :]
        return y * _sigmoid(y)

    def chunk(c, carry):
        rows = pl.ds(pl.multiple_of(c * cl, cl), cl)
        for h in heads:
            xq_s[h, pad:pad + cl, :] = q_ref[rows, cols(h)].astype(F32)
            xk_s[h, pad:pad + cl, :] = k_ref[rows, cols(h)].astype(F32)
        qc = [conv_silu(xq_s, h, cwq_ref, cbq_ref) for h in heads]
        kc = [conv_silu(xk_s, h, cwk_ref, cbk_ref) * (hd ** -0.5) for h in heads]
        qb = [x.astype(BF16) for x in qc]
        kb = [x.astype(BF16) for x in kc]
        vb = [v_ref[rows, cols(h)] for h in heads]

        pre = gt_ref[rows, :] + gb_ref[...]
        bt = _mm(tri_ref[...], _log_sigmoid(pre), precision=HIGHEST)
        b_col = [jnp.sum(bt * sel_f[h], axis=1, keepdims=True) for h in heads]
        i_col = [jnp.sum(pre * sel_i[h], axis=1, keepdims=True) for h in heads]
        g_col = [i_col[h] - b_col[h] for h in heads]
        g_row = [jnp.transpose(jnp.broadcast_to(g_col[h], (cl, LANES)))[0:1, :] for h in heads]

        m_prev = [m_s[h] for h in heads]
        dmat = [jnp.where(tril, b_col[h] + g_row[h], -jnp.inf) for h in heads]
        m_row = [jnp.maximum(b_col[h] + m_prev[h], jnp.max(dmat[h], axis=1, keepdims=True)) for h in heads]
        w_inter = [jnp.exp(b_col[h] + m_prev[h] - m_row[h]) for h in heads]
        qk = [_nt(qb[h], kb[h]) for h in heads]
        s_mat = [qk[h] * jnp.exp(dmat[h] - m_row[h]) for h in heads]
        q_c = [_mm(qb[h], c_s[h].astype(BF16)) for h in heads]
        s_v = [_mm(s_mat[h].astype(BF16), vb[h]) for h in heads]
        den = [w_inter[h] * jnp.sum(qc[h] * n_s[h], axis=1, keepdims=True) + jnp.sum(s_mat[h], axis=1, keepdims=True)
               for h in heads]
        hh = [(w_inter[h] * q_c[h] + s_v[h]) * (1.0 / jnp.maximum(jnp.abs(den[h]), jnp.exp(-m_row[h])))
              for h in heads]

        m_new = [m_row[h][cl - 1:cl, :] for h in heads]
        b_last = [b_col[h][cl - 1:cl, :] for h in heads]
        decay = [jnp.exp(b_last[h] + m_prev[h] - m_new[h]) for h in heads]
        wk = [jnp.exp(b_last[h] + g_col[h] - m_new[h]) * kc[h] for h in heads]
        k_v = [lax.dot_general(wk[h].astype(BF16), vb[h], (((0,), (0,)), ((), ())), preferred_element_type=F32)
               for h in heads]
        for h in heads:
            c_s[h] = decay[h] * c_s[h] + k_v[h]
            n_s[h] = decay[h] * n_s[h] + jnp.sum(wk[h], axis=0, keepdims=True)
            m_s[h] = m_new[h]
        for h in heads:
            hn = hh[h] * lax.rsqrt(jnp.mean(hh[h] * hh[h], axis=1, keepdims=True) + EPS) * ng_ref[:, cols(h)]
            y_ref[rows, cols(h)] = (hn * _sigmoid(og_ref[rows, cols(h)].astype(F32))).astype(BF16)
        return carry

    lax.fori_loop(0, seq // cl, chunk, 0)


def _mlstm(proj, gates, gate_bias, conv_w, conv_b, norm_g, tri, bsz, seq):
    hd = ML_HEAD_DIM
    g = ML_GROUP
    cl = min(ML_CHUNK, seq)
    n_grp = ML_HEADS // g
    col0 = 3 * D_MODEL // (g * hd)
    row_spec = lambda off: pl.BlockSpec((seq, g * hd), lambda b, j: (b, off + j))
    return pl.pallas_call(
        _mlstm_kernel,
        grid=(bsz, n_grp),
        in_specs=[
            row_spec(col0), row_spec(col0 + n_grp), row_spec(col0 + 2 * n_grp), row_spec(col0 + 3 * n_grp),
            pl.BlockSpec((seq, LANES), lambda b, j: (b, 0)),
            pl.BlockSpec((1, LANES), lambda b, j: (0, 0)),
            pl.BlockSpec((CONV_WIDTH, g * hd), lambda b, j: (0, j)),
            pl.BlockSpec((CONV_WIDTH, g * hd), lambda b, j: (0, n_grp + j)),
            pl.BlockSpec((1, g * hd), lambda b, j: (0, j)),
            pl.BlockSpec((1, g * hd), lambda b, j: (0, n_grp + j)),
            pl.BlockSpec((1, g * hd), lambda b, j: (0, j)),
            pl.BlockSpec((cl, cl), lambda b, j: (0, 0)),
        ],
        out_specs=pl.BlockSpec((seq, g * hd), lambda b, j: (b, j)),
        out_shape=jax.ShapeDtypeStruct((bsz * seq, D_MODEL), BF16),
        scratch_shapes=[
            pltpu.VMEM((g, cl + 2 * SUBLANES, hd), F32),
            pltpu.VMEM((g, cl + 2 * SUBLANES, hd), F32),
            pltpu.VMEM((g, hd, hd), F32),
            pltpu.VMEM((g, 1, hd), F32),
            pltpu.VMEM((g, 1, 1), F32),
        ],
        compiler_params=pltpu.CompilerParams(
            dimension_semantics=("arbitrary", "arbitrary"), vmem_limit_bytes=VMEM_LIMIT),
        name="mlstm",
    )(proj, proj, proj, proj, gates, gate_bias, conv_w, conv_w,
      conv_b.reshape(1, -1), conv_b.reshape(1, -1), norm_g.reshape(1, -1), tri)


def _merge_kernel(ya_ref, yb_ref, ga_ref, gb_ref, x_ref, mod_ref, wa_ref, wb_ref, wo_ref, n2_ref, rwt_ref, rb_ref,
                  su_ref, x1_ref, h2_ref, idx_ref, wt_ref, rank_ref, cnt_ref, carry_s):
    @pl.when(pl.program_id(0) == 0)
    def _():
        carry_s[...] = jnp.zeros_like(carry_s)

    ts = x_ref.shape[0] // MERGE_SUB
    subs = range(MERGE_SUB)
    rows = [slice(s * ts, (s + 1) * ts) for s in subs]
    m = mod_ref[0]
    logit = []
    for r in rows:
        a = _mm(ya_ref[r, :], wa_ref[...])
        b = _mm(yb_ref[r, :], wb_ref[...])
        merged = _sigmoid(ga_ref[r, :].astype(F32)) * a + _sigmoid(gb_ref[r, :].astype(F32)) * b
        x1 = x_ref[r, :] + m[2:3] * _mm(merged.astype(BF16), wo_ref[...])
        x1_ref[r, :] = x1
        y = x1 * lax.rsqrt(jnp.mean(x1 * x1, axis=-1, keepdims=True) + EPS) * n2_ref[...]
        h2 = y * (1.0 + m[4:5]) + m[3:4]
        h2_ref[r, :] = h2
        logit.append(_nt(rwt_ref[...], h2, precision=HIGHEST) + rb_ref[...])

    ie = lax.broadcasted_iota(I32, (N_EXPERTS, ts), 0).astype(F32)
    carry = carry_s[:, 0:1]
    wts, idx_out, rank_out = [], [], []
    for s in subs:
        logits = logit[s]
        idxs, vals = [], []
        for _ in range(TOP_K):
            mx = jnp.max(logits, axis=0, keepdims=True)
            am = jnp.min(jnp.where(logits == mx, ie, float(N_EXPERTS)), axis=0, keepdims=True)
            idxs.append(am)
            vals.append(mx)
            logits = jnp.where(ie == am, -jnp.inf, logits)
        exps = [jnp.exp(v - vals[0]) for v in vals]
        inv = 1.0 / (exps[0] + exps[1] + exps[2] + exps[3])
        wts.append(jnp.concatenate([e * inv for e in exps], axis=0))
        idx_out.append(jnp.concatenate(idxs, axis=0).astype(I32))

        onehot = jnp.zeros((N_EXPERTS, ts), F32)
        for am in idxs:
            onehot = onehot + (ie == am).astype(F32)
        before = _mm(onehot.astype(BF16), su_ref[...]) + carry
        ranks = [jnp.sum(jnp.where(ie == am, before, 0.0), axis=0, keepdims=True) for am in idxs]
        rank_out.append(jnp.concatenate(ranks, axis=0).astype(I32))
        carry = carry + jnp.sum(onehot, axis=1, keepdims=True)
    wt_ref[...] = jnp.concatenate(wts, axis=1)
    idx_ref[...] = jnp.concatenate(idx_out, axis=1)
    rank_ref[...] = jnp.concatenate(rank_out, axis=1)
    carry_s[...] = jnp.broadcast_to(carry, carry_s.shape)
    cnt_ref[...] = carry_s[...]


def _merge(ya, yb, proj, x2, mod3, w_a, w_b, w_o, norm2_g, router_wt, router_b, su, seq):
    n_tok = x2.shape[0]
    tm = min(ROW_TILE_MERGE, seq)
    d = D_MODEL
    gcol = 7 * d // d
    row = lambda j: pl.BlockSpec((tm, d), lambda i: (i, j))
    const = lambda shape: pl.BlockSpec(shape, lambda i: tuple(0 for _ in shape))
    tok4 = pl.BlockSpec((TOP_K, tm), lambda i: (0, i))
    return pl.pallas_call(
        _merge_kernel,
        grid=(n_tok // tm,),
        in_specs=[
            row(0), row(0), row(gcol), row(gcol + 1), row(0),
            pl.BlockSpec((1, N_MOD, d), lambda i: ((i * tm) // seq, 0, 0)),
            const((d, d)), const((d, d)), const((d, d)), const((1, d)),
            const((N_EXPERTS, d)), const((N_EXPERTS, 1)), const((tm // MERGE_SUB, tm // MERGE_SUB)),
        ],
        out_specs=[
            row(0), row(0),
            tok4, tok4, tok4,
            const((N_EXPERTS, LANES)),
        ],
        out_shape=[
            jax.ShapeDtypeStruct((n_tok, d), F32),
            jax.ShapeDtypeStruct((n_tok, d), F32),
            jax.ShapeDtypeStruct((TOP_K, n_tok), I32),
            jax.ShapeDtypeStruct((TOP_K, n_tok), F32),
            jax.ShapeDtypeStruct((TOP_K, n_tok), I32),
            jax.ShapeDtypeStruct((N_EXPERTS, LANES), F32),
        ],
        scratch_shapes=[pltpu.VMEM((N_EXPERTS, LANES), F32)],
        compiler_params=pltpu.CompilerParams(dimension_semantics=("arbitrary",), vmem_limit_bytes=VMEM_LIMIT),
        name="merge",
    )(ya, yb, proj, proj, x2, mod3, w_a, w_b, w_o, norm2_g.reshape(1, d), router_wt, router_b.reshape(-1, 1), su)


def _scatter_kernel(fill_lo_ref, fill_hi_ref, nb_ref, dest_ref, h2_ref, xs_ref, zero_s, sem, zsem):
    tm = h2_ref.shape[0]
    bm = zero_s.shape[0]

    @pl.when(pl.program_id(0) == 0)
    def _():
        zero_s[...] = jnp.zeros_like(zero_s)

        def per_expert(e, total):
            lo, hi = fill_lo_ref[e], fill_hi_ref[e]

            def fill(r, c):
                pltpu.make_async_copy(zero_s.at[pl.ds(0, 1), :], xs_ref.at[pl.ds(r, 1), :], sem).start()
                return c

            lax.fori_loop(lo, hi, fill, 0)
            return total + (hi - lo)

        total = lax.fori_loop(0, N_EXPERTS, per_expert, 0)

        def drain(r, c):
            pltpu.make_async_copy(zero_s.at[pl.ds(0, 1), :], xs_ref.at[pl.ds(0, 1), :], sem).wait()
            return c

        lax.fori_loop(0, total, drain, 0)

        n_blocks = xs_ref.shape[0] // bm

        def fill_block(j, c):
            pltpu.make_async_copy(zero_s, xs_ref.at[pl.ds(pl.multiple_of(j * bm, bm), bm), :], zsem).start()
            return c

        lax.fori_loop(nb_ref[0], n_blocks, fill_block, 0)

        def drain_block(j, c):
            pltpu.make_async_copy(zero_s, xs_ref.at[pl.ds(0, bm), :], zsem).wait()
            return c

        lax.fori_loop(nb_ref[0], n_blocks, drain_block, 0)

    def send(g, c):
        for u in range(DMA_UNROLL):
            t = g * DMA_UNROLL + u
            for r in range(TOP_K):
                pltpu.make_async_copy(
                    h2_ref.at[pl.ds(t, 1), :], xs_ref.at[pl.ds(dest_ref[r, t], 1), :], sem).start()
        return c

    lax.fori_loop(0, tm // DMA_UNROLL, send, 0)

    def drain(g, c):
        for _ in range(DMA_UNROLL * TOP_K):
            pltpu.make_async_copy(h2_ref.at[pl.ds(0, 1), :], xs_ref.at[pl.ds(0, 1), :], sem).wait()
        return c

    lax.fori_loop(0, tm // DMA_UNROLL, drain, 0)


def _scatter(fill_lo, fill_hi, n_used, dest, h2s, n_rows, seq):
    n_tok = h2s.shape[0]
    tm = min(MOE_TOKEN_TILE, seq)
    smem = lambda: pl.BlockSpec(memory_space=pltpu.SMEM)
    tok4 = pl.BlockSpec((TOP_K, tm), lambda i: (0, i), memory_space=pltpu.SMEM)
    return pl.pallas_call(
        _scatter_kernel,
        grid=(n_tok // tm,),
        in_specs=[smem(), smem(), smem(), tok4, pl.BlockSpec((tm, D_MODEL), lambda i: (i, 0))],
        out_specs=pl.BlockSpec(memory_space=pl.ANY),
        out_shape=jax.ShapeDtypeStruct((n_rows, D_MODEL), F32),
        scratch_shapes=[pltpu.VMEM((ROW_TILE_MOE, D_MODEL), F32), pltpu.SemaphoreType.DMA,
                        pltpu.SemaphoreType.DMA],
        compiler_params=pltpu.CompilerParams(dimension_semantics=("arbitrary",)),
        name="scatter",
    )(fill_lo, fill_hi, n_used, dest, h2s)


def _expert_kernel(be_ref, nb_ref, xs_ref, w1_ref, b1_ref, w2_ref, b2_ref, ys_ref, w1_s, w2_s):
    j = pl.program_id(0)
    changed = jnp.logical_or(j == 0, be_ref[j] != be_ref[jnp.maximum(j - 1, 0)])

    @pl.when(changed)
    def _():
        w1_s[...] = w1_ref[...].astype(BF16)
        w2_s[...] = w2_ref[...].astype(BF16)

    @pl.when(j < nb_ref[0])
    def _():
        a = _mm(xs_ref[...].astype(BF16), w1_s[...]) + b1_ref[...]
        glu = jnp.minimum(a[:, :D_FF], SWIGLU_LIMIT)
        lin = jnp.clip(a[:, D_FF:], -SWIGLU_LIMIT, SWIGLU_LIMIT)
        hmid = (lin + 1.0) * (glu * _sigmoid(SWIGLU_ALPHA * glu))
        ys_ref[...] = _mm(hmid.astype(BF16), w2_s[...]) + b2_ref[...]

    @pl.when(j >= nb_ref[0])
    def _():
        ys_ref[...] = jnp.zeros_like(ys_ref)


def _experts(block_e, n_blocks_used, xs, w1, b1, w2, b2):
    n_rows = xs.shape[0]
    bm = ROW_TILE_MOE
    d = D_MODEL
    blk = lambda j, be, nb: (j, 0)
    exp = lambda j, be, nb: (be[j], 0, 0)
    return pl.pallas_call(
        _expert_kernel,
        grid_spec=pltpu.PrefetchScalarGridSpec(
            num_scalar_prefetch=2,
            grid=(n_rows // bm,),
            in_specs=[
                pl.BlockSpec((bm, d), blk),
                pl.BlockSpec((None, d, 2 * D_FF), exp),
                pl.BlockSpec((None, 1, 2 * D_FF), exp),
                pl.BlockSpec((None, D_FF, d), exp),
                pl.BlockSpec((None, 1, d), exp),
            ],
            out_specs=pl.BlockSpec((bm, d), blk),
            scratch_shapes=[pltpu.VMEM((d, 2 * D_FF), BF16), pltpu.VMEM((D_FF, d), BF16)],
        ),
        out_shape=jax.ShapeDtypeStruct((n_rows, d), F32),
        compiler_params=pltpu.CompilerParams(dimension_semantics=("arbitrary",), vmem_limit_bytes=VMEM_LIMIT),
        name="experts",
    )(block_e, n_blocks_used, xs, w1, b1.reshape(N_EXPERTS, 1, -1), w2, b2.reshape(N_EXPERTS, 1, -1))


def _combine_kernel(dest_ref, wt_ref, ys_ref, x1_ref, mod_ref, fg_ref, o_ref, buf_s, sem):
    tm = x1_ref.shape[0]

    def fetch(g, c):
        for u in range(DMA_UNROLL):
            t = g * DMA_UNROLL + u
            for r in range(TOP_K):
                pltpu.make_async_copy(
                    ys_ref.at[pl.ds(dest_ref[r, t], 1), :], buf_s.at[r, pl.ds(t, 1), :], sem).start()
        return c

    lax.fori_loop(0, tm // DMA_UNROLL, fetch, 0)

    def drain(g, c):
        for _ in range(DMA_UNROLL * TOP_K):
            pltpu.make_async_copy(ys_ref.at[pl.ds(0, 1), :], buf_s.at[0, pl.ds(0, 1), :], sem).wait()
        return c

    lax.fori_loop(0, tm // DMA_UNROLL, drain, 0)

    wt = wt_ref[...]
    moe = wt[:, 0:1] * buf_s[0]
    for r in range(1, TOP_K):
        moe = moe + wt[:, r:r + 1] * buf_s[r]
    x2 = x1_ref[...] + mod_ref[0][5:6] * moe
    o_ref[...] = x2 * lax.rsqrt(jnp.mean(x2 * x2, axis=-1, keepdims=True) + EPS) * fg_ref[...]


def _combine(dest, wt, ys, x1, mod3, final_g, seq):
    n_tok = x1.shape[0]
    tm = min(MOE_TOKEN_TILE, seq)
    d = D_MODEL
    tok4 = pl.BlockSpec((TOP_K, tm), lambda i: (0, i), memory_space=pltpu.SMEM)
    return pl.pallas_call(
        _combine_kernel,
        grid=(n_tok // tm,),
        in_specs=[
            tok4,
            pl.BlockSpec((tm, TOP_K), lambda i: (i, 0)),
            pl.BlockSpec(memory_space=pl.ANY),
            pl.BlockSpec((tm, d), lambda i: (i, 0)),
            pl.BlockSpec((1, N_MOD, d), lambda i: ((i * tm) // seq, 0, 0)),
            pl.BlockSpec((1, d), lambda i: (0, 0)),
        ],
        out_specs=pl.BlockSpec((tm, d), lambda i: (i, 0)),
        out_shape=jax.ShapeDtypeStruct((n_tok, d), F32),
        scratch_shapes=[pltpu.VMEM((TOP_K, tm, d), F32), pltpu.SemaphoreType.DMA],
        compiler_params=pltpu.CompilerParams(dimension_semantics=("arbitrary",), vmem_limit_bytes=VMEM_LIMIT),
        name="combine",
    )(dest, wt.T, ys, x1, mod3, final_g.reshape(1, d))


def _layer(x2, mod3, bsz, seq, norm1_g, w_in, conv_w, conv_b, ml_b_i, ml_b_f, ml_norm_g, w_branch_a, w_branch_b,
           w_out, norm2_g, router_w, router_b, expert_w1, expert_b1, expert_w2, expert_b2):
    d = D_MODEL
    n_tok = bsz * seq
    gate0 = 7 * d
    n_gate = 2 * ML_HEADS
    w_main = jnp.concatenate([w_in[:, :gate0], w_in[:, gate0 + n_gate:]], axis=1).astype(BF16)
    w_gate = jnp.pad(w_in[:, gate0:gate0 + n_gate], ((0, 0), (0, LANES - n_gate))).astype(BF16)
    gate_bias = jnp.pad(jnp.concatenate([ml_b_i, ml_b_f]), (0, LANES - n_gate)).reshape(1, LANES)

    proj, gates = _proj(x2, mod3, norm1_g, w_main, w_gate, seq)

    ones = jnp.ones((LANES, LANES), F32)
    uu = jnp.concatenate([jnp.tril(ones), ones], axis=1).astype(BF16)
    ya = _sb_attention(proj, uu, bsz, seq)

    cl = min(ML_CHUNK, seq)
    tri = jnp.tril(jnp.ones((cl, cl), F32))
    yb = _mlstm(proj, gates, gate_bias, conv_w, conv_b, ml_norm_g, tri, bsz, seq)

    ts = min(ROW_TILE_MERGE, seq) // MERGE_SUB
    su = jnp.triu(jnp.ones((ts, ts), F32), k=1).astype(BF16)
    x1, h2s, idx, wt, rank, cnt = _merge(
        ya, yb, proj, x2, mod3, w_branch_a.astype(BF16), w_branch_b.astype(BF16), w_out.astype(BF16),
        norm2_g, router_w.T, router_b, su, seq)

    bm = ROW_TILE_MOE
    counts = cnt[:, 0].astype(I32)
    padded = (counts + bm - 1) // bm * bm
    pad_end = jnp.cumsum(padded)
    start = pad_end - padded
    n_rows = (n_tok * TOP_K + N_EXPERTS * (bm - 1)) // bm * bm
    n_blocks = n_rows // bm
    n_used = jnp.maximum(pad_end[-1] // bm, 1).astype(I32)
    blk = jnp.minimum(jnp.arange(n_blocks, dtype=I32), n_used - 1)
    block_e = jnp.minimum(jnp.sum((pad_end[None, :] <= (blk * bm)[:, None]).astype(I32), axis=1), N_EXPERTS - 1)

    onehot = idx[:, :, None] == jnp.arange(N_EXPERTS, dtype=I32)
    dest = rank + jnp.sum(jnp.where(onehot, start, 0), axis=-1)

    xs = _scatter(start + counts, pad_end, n_used.reshape(1), dest, h2s, n_rows, seq)
    ys = _experts(block_e, n_used.reshape(1), xs, expert_w1, expert_b1, expert_w2, expert_b2)
    return dest, wt, ys, x1


def kernel(x, c, ada_w, ada_b, norm1_g, w_in, conv_w, conv_b, ml_b_i, ml_b_f, ml_norm_g, w_branch_a, w_branch_b, w_out, norm2_g, router_w, router_b, expert_w1, expert_b1, expert_w2, expert_b2, final_g):
    bsz, seq, d = x.shape
    depth = ada_w.shape[0]
    assert d == D_MODEL and depth == 1 and seq % QUERY_TILE == 0
    x2 = x.reshape(bsz * seq, d)
    mod3 = _ada(c, ada_w[0], ada_b[0]).reshape(bsz, N_MOD, d)
    dest, wt, ys, x1 = _layer(
        x2, mod3, bsz, seq, norm1_g[0], w_in[0], conv_w[0], conv_b[0], ml_b_i[0], ml_b_f[0], ml_norm_g[0],
        w_branch_a[0], w_branch_b[0], w_out[0], norm2_g[0], router_w[0], router_b[0],
        expert_w1[0], expert_b1[0], expert_w2[0], expert_b2[0])
    out = _combine(dest, wt, ys, x1, mod3, final_g, seq)
    return out.reshape(bsz, seq, d)
```

```python
import functools

import jax
import jax.numpy as jnp
from jax import lax
from jax.experimental import pallas as pl
from jax.experimental.pallas import tpu as pltpu

F32 = jnp.float32
BF16 = jnp.bfloat16
I32 = jnp.int32
HIGHEST = lax.Precision.HIGHEST

D_MODEL = 1024
SB_HEAD_DIM = 64
ML_HEADS = 4
ML_HEAD_DIM = 256
CONV_WIDTH = 4
N_EXPERTS = 32
TOP_K = 4
D_FF = 1024
SWIGLU_LIMIT = 7.0
SWIGLU_ALPHA = 1.702
N_MOD = 6
EPS = 1e-6
LOG2E = 1.4426950408889634

LANES = 128
SUBLANES = 8
DMA_UNROLL = 8

QUERY_TILE = 128
SB_Q_SPAN = 512
SB_K_SPAN = 512
ML_CHUNK = 256
ML_GROUP = 4
ROW_TILE_PROJ = 1024
ROW_TILE_MERGE = 512
MERGE_SUB = 1
ROW_TILE_MOE = 512
MOE_TOKEN_TILE = 512
VMEM_LIMIT = 56 * 1024 * 1024


def _nt(a, b, precision=None):
    return lax.dot_general(a, b, (((1,), (1,)), ((), ())), preferred_element_type=F32, precision=precision)


def _mm(a, b, precision=None):
    return jnp.dot(a, b, preferred_element_type=F32, precision=precision)


def _sigmoid(x):
    return 1.0 / (1.0 + jnp.exp(-x))


def _log_sigmoid(x):
    return jnp.minimum(x, 0.0) - jnp.log(1.0 + jnp.exp(-jnp.abs(x)))


def _ada_kernel(c_ref, w_ref, b_ref, o_ref):
    c = c_ref[...]
    o_ref[...] = _mm(c * _sigmoid(c), w_ref[...], precision=HIGHEST) + b_ref[...]


def _ada(c, ada_w, ada_b):
    bsz = c.shape[0]
    n = ada_w.shape[1]
    return pl.pallas_call(
        _ada_kernel,
        grid=(n // D_MODEL,),
        in_specs=[
            pl.BlockSpec((bsz, D_MODEL), lambda j: (0, 0)),
            pl.BlockSpec((D_MODEL, D_MODEL), lambda j: (0, j)),
            pl.BlockSpec((1, D_MODEL), lambda j: (0, j)),
        ],
        out_specs=pl.BlockSpec((bsz, D_MODEL), lambda j: (0, j)),
        out_shape=jax.ShapeDtypeStruct((bsz, n), F32),
        name="ada",
    )(c, ada_w, ada_b.reshape(1, n))


def _proj_kernel(x_ref, mod_ref, g_ref, w_ref, wg_ref, o_ref, og_ref, h_ref):
    @pl.when(pl.program_id(1) == 0)
    def _():
        x = x_ref[...]
        y = x * lax.rsqrt(jnp.mean(x * x, axis=-1, keepdims=True) + EPS) * g_ref[...]
        m = mod_ref[0]
        hb = (y * (1.0 + m[1:2]) + m[0:1]).astype(BF16)
        h_ref[...] = hb
        og_ref[...] = _mm(hb, wg_ref[...])

    o_ref[...] = _mm(h_ref[...], w_ref[...]).astype(BF16)


def _proj(x2, mod3, norm_g, w_main, w_gate, seq):
    n_tok = x2.shape[0]
    n_main = w_main.shape[1]
    tm = min(ROW_TILE_PROJ, seq)
    tn = D_MODEL
    return pl.pallas_call(
        _proj_kernel,
        grid=(n_tok // tm, n_main // tn),
        in_specs=[
            pl.BlockSpec((tm, D_MODEL), lambda i, j: (i, 0)),
            pl.BlockSpec((1, N_MOD, D_MODEL), lambda i, j: ((i * tm) // seq, 0, 0)),
            pl.BlockSpec((1, D_MODEL), lambda i, j: (0, 0)),
            pl.BlockSpec((D_MODEL, tn), lambda i, j: (0, j)),
            pl.BlockSpec((D_MODEL, LANES), lambda i, j: (0, 0)),
        ],
        out_specs=[
            pl.BlockSpec((tm, tn), lambda i, j: (i, j)),
            pl.BlockSpec((tm, LANES), lambda i, j: (i, 0)),
        ],
        out_shape=[
            jax.ShapeDtypeStruct((n_tok, n_main), BF16),
            jax.ShapeDtypeStruct((n_tok, LANES), F32),
        ],
        scratch_shapes=[pltpu.VMEM((tm, D_MODEL), BF16)],
        compiler_params=pltpu.CompilerParams(
            dimension_semantics=("arbitrary", "arbitrary"), vmem_limit_bytes=VMEM_LIMIT),
        name="proj",
    )(x2, mod3, norm_g.reshape(1, D_MODEL), w_main, w_gate)


def _sb_kernel(q_ref, k_ref, v_ref, uu_ref, o_ref, qh_s, run_s, acc_s):
    seq = q_ref.shape[0]
    qt = QUERY_TILE
    tq, tk = min(SB_Q_SPAN, seq), min(SB_K_SPAN, seq)
    n_heads = LANES // SB_HEAD_DIM
    n_qb, n_kb = tq // qt, tk // qt
    lane = lax.broadcasted_iota(I32, (qt, LANES), 1)
    strict = lane < lax.broadcasted_iota(I32, (qt, LANES), 0)
    uu = uu_ref[...]

    def group(s0, diag):
        k = k_ref[pl.ds(s0, tk), :]
        v = v_ref[pl.ds(s0, tk), :]
        chains = range(n_qb * n_heads)
        nbs = [ci // n_heads + 1 if diag else n_kb for ci in chains]
        blk = lambda x, b: x[:, b * qt:(b + 1) * qt]
        zs = [_nt(qh_s[ci], k[:nbs[ci] * qt]) for ci in chains]
        lkb = []
        for ci in chains:
            z = zs[ci]
            nz = -z
            lk = jnp.minimum(nz, 0.0) - jnp.log(1.0 + jnp.exp2(jnp.minimum(z, nz))) * LOG2E
            lks = [blk(lk, b) for b in range(nbs[ci])]
            if diag:
                lks[-1] = jnp.where(strict, lks[-1], 0.0)
            lkb.append([x.astype(BF16) for x in lks])
        ccs = [[_mm(lkb[ci][b], uu) for b in range(nbs[ci])] for ci in chains]
        probs = []
        for ci in chains:
            run = run_s[ci]
            a = [None] * nbs[ci]
            for b in reversed(range(nbs[ci])):
                a[b] = jnp.exp2(blk(zs[ci], b) + ccs[ci][b][:, :LANES] + run)
                run = run + ccs[ci][b][:, LANES:]
            run_s[ci] = run
            if diag:
                a[-1] = jnp.where(strict, a[-1], 0.0)
            probs.append(jnp.concatenate(a, axis=1).astype(BF16))
        for ci in chains:
            acc_s[ci] = acc_s[ci] + _mm(probs[ci], v[:nbs[ci] * qt])

    def qspan(i, carry):
        r0 = pl.multiple_of(i * tq, tq)
        run_s[...] = jnp.zeros_like(run_s)
        acc_s[...] = jnp.zeros_like(acc_s)
        for qb in range(n_qb):
            qf = q_ref[pl.ds(r0 + qb * qt, qt), :].astype(F32) * (SB_HEAD_DIM ** -0.5 * LOG2E)
            for h in range(n_heads):
                in_head = (lane >= h * SB_HEAD_DIM) & (lane < (h + 1) * SB_HEAD_DIM)
                qh_s[qb * n_heads + h] = jnp.where(in_head, qf, 0.0).astype(BF16)

        group(r0, True)

        def past(j, c):
            group(pl.multiple_of((i - 1 - j) * tk, tk), False)
            return c

        lax.fori_loop(0, i, past, 0)
        for qb in range(n_qb):
            o_ref[pl.ds(r0 + qb * qt, qt), :] = jnp.where(
                lane < SB_HEAD_DIM, acc_s[qb * n_heads], acc_s[qb * n_heads + 1]).astype(BF16)
        return carry

    lax.fori_loop(0, seq // tq, qspan, 0)


def _sb_attention(proj, uu, bsz, seq):
    n_pairs = D_MODEL // LANES
    tq, tk = min(SB_Q_SPAN, seq), min(SB_K_SPAN, seq)
    assert tk == tq and seq % tk == 0 and LANES // SB_HEAD_DIM == 2
    n_chain = (tq // QUERY_TILE) * (LANES // SB_HEAD_DIM)
    return pl.pallas_call(
        _sb_kernel,
        grid=(bsz, n_pairs),
        in_specs=[
            pl.BlockSpec((seq, LANES), lambda b, p: (b, p)),
            pl.BlockSpec((seq, LANES), lambda b, p: (b, n_pairs + p)),
            pl.BlockSpec((seq, LANES), lambda b, p: (b, 2 * n_pairs + p)),
            pl.BlockSpec((LANES, 2 * LANES), lambda b, p: (0, 0)),
        ],
        out_specs=pl.BlockSpec((seq, LANES), lambda b, p: (b, p)),
        out_shape=jax.ShapeDtypeStruct((bsz * seq, D_MODEL), BF16),
        scratch_shapes=[
            pltpu.VMEM((n_chain, QUERY_TILE, LANES), BF16),
            pltpu.VMEM((n_chain, QUERY_TILE, LANES), F32),
            pltpu.VMEM((n_chain, QUERY_TILE, LANES), F32),
        ],
        compiler_params=pltpu.CompilerParams(dimension_semantics=("arbitrary", "arbitrary")),
        name="sb_attn",
    )(proj, proj, proj, uu)


def _mlstm_kernel(q_ref, k_ref, v_ref, og_ref, gt_ref, gb_ref, cwq_ref, cwk_ref, cbq_ref, cbk_ref, ng_ref, tri_ref,
                  y_ref, xq_s, xk_s, c_s, n_s, m_s):
    hd = ML_HEAD_DIM
    heads = range(ML_GROUP)
    first = pl.program_id(1) * ML_GROUP
    seq = q_ref.shape[0]
    cl = min(ML_CHUNK, seq)
    pad = SUBLANES
    for h in heads:
        xq_s[h, 0:pad, :] = jnp.zeros((pad, hd), F32)
        xk_s[h, 0:pad, :] = jnp.zeros((pad, hd), F32)
    c_s[...] = jnp.zeros_like(c_s)
    n_s[...] = jnp.zeros_like(n_s)
    m_s[...] = jnp.zeros_like(m_s)
    lane = lax.broadcasted_iota(I32, (cl, LANES), 1)
    sel_i = [(lane == first + h).astype(F32) for h in heads]
    sel_f = [(lane == first + h + ML_HEADS).astype(F32) for h in heads]
    tril = lax.broadcasted_iota(I32, (cl, cl), 1) <= lax.broadcasted_iota(I32, (cl, cl), 0)
    cols = lambda h: slice(h * hd, (h + 1) * hd)

    def conv_silu(x_s, h, cw_ref, cb_ref):
        y = cb_ref[:, cols(h)]
        for tap in range(CONV_WIDTH):
            lo = pad - (CONV_WIDTH - 1) + tap
            y = y + x_s[h, lo:lo + cl, :] * cw_ref[tap:tap + 1, cols(h)]
        x_s[h, 0:pad, :] = x_s[h, cl:cl + pad, :]
        return y * _sigmoid(y)

    def chunk(c, carry):
        rows = pl.ds(pl.multiple_of(c * cl, cl), cl)
        for h in heads:
            xq_s[h, pad:pad + cl, :] = q_ref[rows, cols(h)].astype(F32)
            xk_s[h, pad:pad + cl, :] = k_ref[rows, cols(h)].astype(F32)
        qc = [conv_silu(xq_s, h, cwq_ref, cbq_ref) for h in heads]
        kc = [conv_silu(xk_s, h, cwk_ref, cbk_ref) * (hd ** -0.5) for h in heads]
        qb = [x.astype(BF16) for x in qc]
        kb = [x.astype(BF16) for x in kc]
        vb = [v_ref[rows, cols(h)] for h in heads]

        pre = gt_ref[rows, :] + gb_ref[...]
        bt = _mm(tri_ref[...], _log_sigmoid(pre), precision=HIGHEST)
        b_col = [jnp.sum(bt * sel_f[h], axis=1, keepdims=True) for h in heads]
        i_col = [jnp.sum(pre * sel_i[h], axis=1, keepdims=True) for h in heads]
        g_col = [i_col[h] - b_col[h] for h in heads]
        g_row = [jnp.transpose(jnp.broadcast_to(g_col[h], (cl, LANES)))[0:1, :] for h in heads]

        m_prev = [m_s[h] for h in heads]
        dmat = [jnp.where(tril, b_col[h] + g_row[h], -jnp.inf) for h in heads]
        m_row = [jnp.maximum(b_col[h] + m_prev[h], jnp.max(dmat[h], axis=1, keepdims=True)) for h in heads]
        w_inter = [jnp.exp(b_col[h] + m_prev[h] - m_row[h]) for h in heads]
        qk = [_nt(qb[h], kb[h]) for h in heads]
        s_mat = [qk[h] * jnp.exp(dmat[h] - m_row[h]) for h in heads]
        q_c = [_mm(qb[h], c_s[h].astype(BF16)) for h in heads]
        s_v = [_mm(s_mat[h].astype(BF16), vb[h]) for h in heads]
        den = [w_inter[h] * jnp.sum(qc[h] * n_s[h], axis=1, keepdims=True) + jnp.sum(s_mat[h], axis=1, keepdims=True)
               for h in heads]
        hh = [(w_inter[h] * q_c[h] + s_v[h]) * (1.0 / jnp.maximum(jnp.abs(den[h]), jnp.exp(-m_row[h])))
              for h in heads]

        m_new = [m_row[h][cl - 1:cl, :] for h in heads]
        b_last = [b_col[h][cl - 1:cl, :] for h in heads]
        decay = [jnp.exp(b_last[h] + m_prev[h] - m_new[h]) for h in heads]
        wk = [jnp.exp(b_last[h] + g_col[h] - m_new[h]) * kc[h] for h in heads]
        k_v = [lax.dot_general(wk[h].astype(BF16), vb[h], (((0,), (0,)), ((), ())), preferred_element_type=F32)
               for h in heads]
        for h in heads:
            c_s[h] = decay[h] * c_s[h] + k_v[h]
            n_s[h] = decay[h] * n_s[h] + jnp.sum(wk[h], axis=0, keepdims=True)
            m_s[h] = m_new[h]
        for h in heads:
            hn = hh[h] * lax.rsqrt(jnp.mean(hh[h] * hh[h], axis=1, keepdims=True) + EPS) * ng_ref[:, cols(h)]
            y_ref[rows, cols(h)] = (hn * _sigmoid(og_ref[rows, cols(h)].astype(F32))).astype(BF16)
        return carry

    lax.fori_loop(0, seq // cl, chunk, 0)


def _mlstm(proj, gates, gate_bias, conv_w, conv_b, norm_g, tri, bsz, seq):
    hd = ML_HEAD_DIM
    g = ML_GROUP
    cl = min(ML_CHUNK, seq)
    n_grp = ML_HEADS // g
    col0 = 3 * D_MODEL // (g * hd)
    row_spec = lambda off: pl.BlockSpec((seq, g * hd), lambda b, j: (b, off + j))
    return pl.pallas_call(
        _mlstm_kernel,
        grid=(bsz, n_grp),
        in_specs=[
            row_spec(col0), row_spec(col0 + n_grp), row_spec(col0 + 2 * n_grp), row_spec(col0 + 3 * n_grp),
            pl.BlockSpec((seq, LANES), lambda b, j: (b, 0)),
            pl.BlockSpec((1, LANES), lambda b, j: (0, 0)),
            pl.BlockSpec((CONV_WIDTH, g * hd), lambda b, j: (0, j)),
            pl.BlockSpec((CONV_WIDTH, g * hd), lambda b, j: (0, n_grp + j)),
            pl.BlockSpec((1, g * hd), lambda b, j: (0, j)),
            pl.BlockSpec((1, g * hd), lambda b, j: (0, n_grp + j)),
            pl.BlockSpec((1, g * hd), lambda b, j: (0, j)),
            pl.BlockSpec((cl, cl), lambda b, j: (0, 0)),
        ],
        out_specs=pl.BlockSpec((seq, g * hd), lambda b, j: (b, j)),
        out_shape=jax.ShapeDtypeStruct((bsz * seq, D_MODEL), BF16),
        scratch_shapes=[
            pltpu.VMEM((g, cl + 2 * SUBLANES, hd), F32),
            pltpu.VMEM((g, cl + 2 * SUBLANES, hd), F32),
            pltpu.VMEM((g, hd, hd), F32),
            pltpu.VMEM((g, 1, hd), F32),
            pltpu.VMEM((g, 1, 1), F32),
        ],
        compiler_params=pltpu.CompilerParams(
            dimension_semantics=("arbitrary", "arbitrary"), vmem_limit_bytes=VMEM_LIMIT),
        name="mlstm",
    )(proj, proj, proj, proj, gates, gate_bias, conv_w, conv_w,
      conv_b.reshape(1, -1), conv_b.reshape(1, -1), norm_g.reshape(1, -1), tri)


def _merge_kernel(ya_ref, yb_ref, ga_ref, gb_ref, x_ref, mod_ref, wa_ref, wb_ref, wo_ref, n2_ref, rwt_ref, rb_ref,
                  su_ref, x1_ref, h2_ref, idx_ref, wt_ref, rank_ref, cnt_ref, carry_s):
    @pl.when(pl.program_id(0) == 0)
    def _():
        carry_s[...] = jnp.zeros_like(carry_s)

    ts = x_ref.shape[0] // MERGE_SUB
    subs = range(MERGE_SUB)
    rows = [slice(s * ts, (s + 1) * ts) for s in subs]
    m = mod_ref[0]
    logit = []
    for r in rows:
        a = _mm(ya_ref[r, :], wa_ref[...])
        b = _mm(yb_ref[r, :], wb_ref[...])
        merged = _sigmoid(ga_ref[r, :].astype(F32)) * a + _sigmoid(gb_ref[r, :].astype(F32)) * b
        x1 = x_ref[r, :] + m[2:3] * _mm(merged.astype(BF16), wo_ref[...])
        x1_ref[r, :] = x1
        y = x1 * lax.rsqrt(jnp.mean(x1 * x1, axis=-1, keepdims=True) + EPS) * n2_ref[...]
        h2 = y * (1.0 + m[4:5]) + m[3:4]
        h2_ref[r, :] = h2
        logit.append(_nt(rwt_ref[...], h2, precision=HIGHEST) + rb_ref[...])

    ie = lax.broadcasted_iota(I32, (N_EXPERTS, ts), 0).astype(F32)
    carry = carry_s[:, 0:1]
    wts, idx_out, rank_out = [], [], []
    for s in subs:
        logits = logit[s]
        idxs, vals = [], []
        for _ in range(TOP_K):
            mx = jnp.max(logits, axis=0, keepdims=True)
            am = jnp.min(jnp.where(logits == mx, ie, float(N_EXPERTS)), axis=0, keepdims=True)
            idxs.append(am)
            vals.append(mx)
            logits = jnp.where(ie == am, -jnp.inf, logits)
        exps = [jnp.exp(v - vals[0]) for v in vals]
        inv = 1.0 / (exps[0] + exps[1] + exps[2] + exps[3])
        wts.append(jnp.concatenate([e * inv for e in exps], axis=0))
        idx_out.append(jnp.concatenate(idxs, axis=0).astype(I32))

        onehot = jnp.zeros((N_EXPERTS, ts), F32)
        for am in idxs:
            onehot = onehot + (ie == am).astype(F32)
        before = _mm(onehot.astype(BF16), su_ref[...]) + carry
        ranks = [jnp.sum(jnp.where(ie == am, before, 0.0), axis=0, keepdims=True) for am in idxs]
        rank_out.append(jnp.concatenate(ranks, axis=0).astype(I32))
        carry = carry + jnp.sum(onehot, axis=1, keepdims=True)
    wt_ref[...] = jnp.concatenate(wts, axis=1)
    idx_ref[...] = jnp.concatenate(idx_out, axis=1)
    rank_ref[...] = jnp.concatenate(rank_out, axis=1)
    carry_s[...] = jnp.broadcast_to(carry, carry_s.shape)
    cnt_ref[...] = carry_s[...]


def _merge(ya, yb, proj, x2, mod3, w_a, w_b, w_o, norm2_g, router_wt, router_b, su, seq):
    n_tok = x2.shape[0]
    tm = min(ROW_TILE_MERGE, seq)
    d = D_MODEL
    gcol = 7 * d // d
    row = lambda j: pl.BlockSpec((tm, d), lambda i: (i, j))
    const = lambda shape: pl.BlockSpec(shape, lambda i: tuple(0 for _ in shape))
    tok4 = pl.BlockSpec((TOP_K, tm), lambda i: (0, i))
    return pl.pallas_call(
        _merge_kernel,
        grid=(n_tok // tm,),
        in_specs=[
            row(0), row(0), row(gcol), row(gcol + 1), row(0),
            pl.BlockSpec((1, N_MOD, d), lambda i: ((i * tm) // seq, 0, 0)),
            const((d, d)), const((d, d)), const((d, d)), const((1, d)),
            const((N_EXPERTS, d)), const((N_EXPERTS, 1)), const((tm // MERGE_SUB, tm // MERGE_SUB)),
        ],
        out_specs=[
            row(0), row(0),
            tok4, tok4, tok4,
            const((N_EXPERTS, LANES)),
        ],
        out_shape=[
            jax.ShapeDtypeStruct((n_tok, d), F32),
            jax.ShapeDtypeStruct((n_tok, d), F32),
            jax.ShapeDtypeStruct((TOP_K, n_tok), I32),
            jax.ShapeDtypeStruct((TOP_K, n_tok), F32),
            jax.ShapeDtypeStruct((TOP_K, n_tok), I32),
            jax.ShapeDtypeStruct((N_EXPERTS, LANES), F32),
        ],
        scratch_shapes=[pltpu.VMEM((N_EXPERTS, LANES), F32)],
        compiler_params=pltpu.CompilerParams(dimension_semantics=("arbitrary",), vmem_limit_bytes=VMEM_LIMIT),
        name="merge",
    )(ya, yb, proj, proj, x2, mod3, w_a, w_b, w_o, norm2_g.reshape(1, d), router_wt, router_b.reshape(-1, 1), su)


def _scatter_kernel(start_ref, end_ref, nb_ref, dest_ref, h2_ref, xs_ref, zero_s, sem, zsem):
    tm = h2_ref.shape[0]
    bm = zero_s.shape[0]

    @pl.when(pl.program_id(0) == 0)
    def _():
        zero_s[...] = jnp.zeros_like(zero_s)

        def zero_block(r0):
            pltpu.make_async_copy(zero_s, xs_ref.at[pl.ds(pl.multiple_of(r0, bm), bm), :], zsem).start()

        def per_expert(e, n):
            lo, hi = start_ref[e], end_ref[e]

            @pl.when(hi > lo)
            def _():
                zero_block(hi - bm)

            return n + (hi > lo).astype(I32)

        n_fill = lax.fori_loop(0, N_EXPERTS, per_expert, 0)
        n_blocks = xs_ref.shape[0] // bm

        def fill_block(j, c):
            zero_block(j * bm)
            return c

        lax.fori_loop(nb_ref[0], n_blocks, fill_block, 0)

        def drain_block(j, c):
            pltpu.make_async_copy(zero_s, xs_ref.at[pl.ds(0, bm), :], zsem).wait()
            return c

        lax.fori_loop(0, n_fill + n_blocks - nb_ref[0], drain_block, 0)

    def send(g, c):
        for u in range(DMA_UNROLL):
            t = g * DMA_UNROLL + u
            for r in range(TOP_K):
                pltpu.make_async_copy(
                    h2_ref.at[pl.ds(t, 1), :], xs_ref.at[pl.ds(dest_ref[r, t], 1), :], sem).start()
        return c

    lax.fori_loop(0, tm // DMA_UNROLL, send, 0)

    def drain(g, c):
        for _ in range(DMA_UNROLL * TOP_K):
            pltpu.make_async_copy(h2_ref.at[pl.ds(0, 1), :], xs_ref.at[pl.ds(0, 1), :], sem).wait()
        return c

    lax.fori_loop(0, tm // DMA_UNROLL, drain, 0)


def _scatter(start, end, n_used, dest, h2s, n_rows, seq):
    n_tok = h2s.shape[0]
    tm = min(MOE_TOKEN_TILE, seq)
    smem = lambda: pl.BlockSpec(memory_space=pltpu.SMEM)
    tok4 = pl.BlockSpec((TOP_K, tm), lambda i: (0, i), memory_space=pltpu.SMEM)
    return pl.pallas_call(
        _scatter_kernel,
        grid=(n_tok // tm,),
        in_specs=[smem(), smem(), smem(), tok4, pl.BlockSpec((tm, D_MODEL), lambda i: (i, 0))],
        out_specs=pl.BlockSpec(memory_space=pl.ANY),
        out_shape=jax.ShapeDtypeStruct((n_rows, D_MODEL), F32),
        scratch_shapes=[pltpu.VMEM((ROW_TILE_MOE, D_MODEL), F32), pltpu.SemaphoreType.DMA,
                        pltpu.SemaphoreType.DMA],
        compiler_params=pltpu.CompilerParams(dimension_semantics=("arbitrary",)),
        name="scatter",
    )(start, end, n_used, dest, h2s)


def _expert_kernel(be_ref, nb_ref, xs_ref, w1_ref, b1_ref, w2_ref, b2_ref, ys_ref, w1_s, w2_s):
    j = pl.program_id(0)
    changed = jnp.logical_or(j == 0, be_ref[j] != be_ref[jnp.maximum(j - 1, 0)])

    @pl.when(changed)
    def _():
        w1_s[...] = w1_ref[...].astype(BF16)
        w2_s[...] = w2_ref[...].astype(BF16)

    @pl.when(j < nb_ref[0])
    def _():
        a = _mm(xs_ref[...].astype(BF16), w1_s[...]) + b1_ref[...]
        glu = jnp.minimum(a[:, :D_FF], SWIGLU_LIMIT)
        lin = jnp.clip(a[:, D_FF:], -SWIGLU_LIMIT, SWIGLU_LIMIT)
        hmid = (lin + 1.0) * (glu * _sigmoid(SWIGLU_ALPHA * glu))
        ys_ref[...] = _mm(hmid.astype(BF16), w2_s[...]) + b2_ref[...]

    @pl.when(j >= nb_ref[0])
    def _():
        ys_ref[...] = jnp.zeros_like(ys_ref)


def _experts(block_e, n_blocks_used, xs, w1, b1, w2, b2):
    n_rows = xs.shape[0]
    bm = ROW_TILE_MOE
    d = D_MODEL
    blk = lambda j, be, nb: (j, 0)
    exp = lambda j, be, nb: (be[j], 0, 0)
    return pl.pallas_call(
        _expert_kernel,
        grid_spec=pltpu.PrefetchScalarGridSpec(
            num_scalar_prefetch=2,
            grid=(n_rows // bm,),
            in_specs=[
                pl.BlockSpec((bm, d), blk),
                pl.BlockSpec((None, d, 2 * D_FF), exp),
                pl.BlockSpec((None, 1, 2 * D_FF), exp),
                pl.BlockSpec((None, D_FF, d), exp),
                pl.BlockSpec((None, 1, d), exp),
            ],
            out_specs=pl.BlockSpec((bm, d), blk),
            scratch_shapes=[pltpu.VMEM((d, 2 * D_FF), BF16), pltpu.VMEM((D_FF, d), BF16)],
        ),
        out_shape=jax.ShapeDtypeStruct((n_rows, d), F32),
        compiler_params=pltpu.CompilerParams(dimension_semantics=("arbitrary",), vmem_limit_bytes=VMEM_LIMIT),
        name="experts",
    )(block_e, n_blocks_used, xs, w1, b1.reshape(N_EXPERTS, 1, -1), w2, b2.reshape(N_EXPERTS, 1, -1))


def _combine_kernel(dest_ref, wt_ref, ys_ref, x1_ref, mod_ref, fg_ref, o_ref, buf_s, sem):
    tm = x1_ref.shape[0]

    def fetch(g, c):
        for u in range(DMA_UNROLL):
            t = g * DMA_UNROLL + u
            for r in range(TOP_K):
                pltpu.make_async_copy(
                    ys_ref.at[pl.ds(dest_ref[r, t], 1), :], buf_s.at[r, pl.ds(t, 1), :], sem).start()
        return c

    lax.fori_loop(0, tm // DMA_UNROLL, fetch, 0)

    def drain(g, c):
        for _ in range(DMA_UNROLL * TOP_K):
            pltpu.make_async_copy(ys_ref.at[pl.ds(0, 1), :], buf_s.at[0, pl.ds(0, 1), :], sem).wait()
        return c

    lax.fori_loop(0, tm // DMA_UNROLL, drain, 0)

    wt = wt_ref[...]
    moe = wt[:, 0:1] * buf_s[0]
    for r in range(1, TOP_K):
        moe = moe + wt[:, r:r + 1] * buf_s[r]
    x2 = x1_ref[...] + mod_ref[0][5:6] * moe
    o_ref[...] = x2 * lax.rsqrt(jnp.mean(x2 * x2, axis=-1, keepdims=True) + EPS) * fg_ref[...]


def _combine(dest, wt, ys, x1, mod3, final_g, seq):
    n_tok = x1.shape[0]
    tm = min(MOE_TOKEN_TILE, seq)
    d = D_MODEL
    tok4 = pl.BlockSpec((TOP_K, tm), lambda i: (0, i), memory_space=pltpu.SMEM)
    return pl.pallas_call(
        _combine_kernel,
        grid=(n_tok // tm,),
        in_specs=[
            tok4,
            pl.BlockSpec((tm, TOP_K), lambda i: (i, 0)),
            pl.BlockSpec(memory_space=pl.ANY),
            pl.BlockSpec((tm, d), lambda i: (i, 0)),
            pl.BlockSpec((1, N_MOD, d), lambda i: ((i * tm) // seq, 0, 0)),
            pl.BlockSpec((1, d), lambda i: (0, 0)),
        ],
        out_specs=pl.BlockSpec((tm, d), lambda i: (i, 0)),
        out_shape=jax.ShapeDtypeStruct((n_tok, d), F32),
        scratch_shapes=[pltpu.VMEM((TOP_K, tm, d), F32), pltpu.SemaphoreType.DMA],
        compiler_params=pltpu.CompilerParams(dimension_semantics=("arbitrary",), vmem_limit_bytes=VMEM_LIMIT),
        name="combine",
    )(dest, wt.T, ys, x1, mod3, final_g.reshape(1, d))


def _layer(x2, mod3, bsz, seq, norm1_g, w_in, conv_w, conv_b, ml_b_i, ml_b_f, ml_norm_g, w_branch_a, w_branch_b,
           w_out, norm2_g, router_w, router_b, expert_w1, expert_b1, expert_w2, expert_b2):
    d = D_MODEL
    n_tok = bsz * seq
    gate0 = 7 * d
    n_gate = 2 * ML_HEADS
    w_main = jnp.concatenate([w_in[:, :gate0], w_in[:, gate0 + n_gate:]], axis=1).astype(BF16)
    w_gate = jnp.pad(w_in[:, gate0:gate0 + n_gate], ((0, 0), (0, LANES - n_gate))).astype(BF16)
    gate_bias = jnp.pad(jnp.concatenate([ml_b_i, ml_b_f]), (0, LANES - n_gate)).reshape(1, LANES)

    proj, gates = _proj(x2, mod3, norm1_g, w_main, w_gate, seq)

    ones = jnp.ones((LANES, LANES), F32)
    uu = jnp.concatenate([jnp.tril(ones), ones], axis=1).astype(BF16)
    ya = _sb_attention(proj, uu, bsz, seq)

    cl = min(ML_CHUNK, seq)
    tri = jnp.tril(jnp.ones((cl, cl), F32))
    yb = _mlstm(proj, gates, gate_bias, conv_w, conv_b, ml_norm_g, tri, bsz, seq)

    ts = min(ROW_TILE_MERGE, seq) // MERGE_SUB
    su = jnp.triu(jnp.ones((ts, ts), F32), k=1).astype(BF16)
    x1, h2s, idx, wt, rank, cnt = _merge(
        ya, yb, proj, x2, mod3, w_branch_a.astype(BF16), w_branch_b.astype(BF16), w_out.astype(BF16),
        norm2_g, router_w.T, router_b, su, seq)

    bm = ROW_TILE_MOE
    counts = cnt[:, 0].astype(I32)
    padded = (counts + bm - 1) // bm * bm
    pad_end = jnp.cumsum(padded)
    start = pad_end - padded
    n_rows = (n_tok * TOP_K + N_EXPERTS * (bm - 1)) // bm * bm
    n_blocks = n_rows // bm
    n_used = jnp.maximum(pad_end[-1] // bm, 1).astype(I32)
    blk = jnp.minimum(jnp.arange(n_blocks, dtype=I32), n_used - 1)
    block_e = jnp.minimum(jnp.sum((pad_end[None, :] <= (blk * bm)[:, None]).astype(I32), axis=1), N_EXPERTS - 1)

    onehot = idx[:, :, None] == jnp.arange(N_EXPERTS, dtype=I32)
    dest = rank + jnp.sum(jnp.where(onehot, start, 0), axis=-1)

    xs = _scatter(start, pad_end, n_used.reshape(1), dest, h2s, n_rows, seq)
    ys = _experts(block_e, n_used.reshape(1), xs, expert_w1, expert_b1, expert_w2, expert_b2)
    return dest, wt, ys, x1


def kernel(x, c, ada_w, ada_b, norm1_g, w_in, conv_w, conv_b, ml_b_i, ml_b_f, ml_norm_g, w_branch_a, w_branch_b, w_out, norm2_g, router_w, router_b, expert_w1, expert_b1, expert_w2, expert_b2, final_g):
    bsz, seq, d = x.shape
    depth = ada_w.shape[0]
    assert d == D_MODEL and depth == 1 and seq % QUERY_TILE == 0
    x2 = x.reshape(bsz * seq, d)
    mod3 = _ada(c, ada_w[0], ada_b[0]).reshape(bsz, N_MOD, d)
    dest, wt, ys, x1 = _layer(
        x2, mod3, bsz, seq, norm1_g[0], w_in[0], conv_w[0], conv_b[0], ml_b_i[0], ml_b_f[0], ml_norm_g[0],
        w_branch_a[0], w_branch_b[0], w_out[0], norm2_g[0], router_w[0], router_b[0],
        expert_w1[0], expert_b1[0], expert_w2[0], expert_b2[0])
    out = _combine(dest, wt, ys, x1, mod3, final_g, seq)
    return out.reshape(bsz, seq, d)
```

```python
import functools

import jax
import jax.numpy as jnp
from jax import lax
from jax.experimental import pallas as pl
from jax.experimental.pallas import tpu as pltpu

F32 = jnp.float32
BF16 = jnp.bfloat16
I32 = jnp.int32
HIGHEST = lax.Precision.HIGHEST

D_MODEL = 1024
SB_HEAD_DIM = 64
ML_HEADS = 4
ML_HEAD_DIM = 256
CONV_WIDTH = 4
N_EXPERTS = 32
TOP_K = 4
D_FF = 1024
SWIGLU_LIMIT = 7.0
SWIGLU_ALPHA = 1.702
N_MOD = 6
EPS = 1e-6
LOG2E = 1.4426950408889634

LANES = 128
SUBLANES = 8
DMA_UNROLL = 8

QUERY_TILE = 128
SB_Q_SPAN = 512
SB_K_SPAN = 512
ML_CHUNK = 256
ML_GROUP = 4
ROW_TILE_PROJ = 1024
ROW_TILE_MERGE = 512
MERGE_SUB = 1
ROW_TILE_MOE = 512
MOE_TOKEN_TILE = 512
VMEM_LIMIT = 56 * 1024 * 1024


def _nt(a, b, precision=None):
    return lax.dot_general(a, b, (((1,), (1,)), ((), ())), preferred_element_type=F32, precision=precision)


def _mm(a, b, precision=None):
    return jnp.dot(a, b, preferred_element_type=F32, precision=precision)


def _sigmoid(x):
    return 1.0 / (1.0 + jnp.exp(-x))


def _log_sigmoid(x):
    return jnp.minimum(x, 0.0) - jnp.log(1.0 + jnp.exp(-jnp.abs(x)))


def _ada_kernel(c_ref, w_ref, b_ref, o_ref):
    c = c_ref[...]
    o_ref[...] = _mm(c * _sigmoid(c), w_ref[...], precision=HIGHEST) + b_ref[...]


def _ada(c, ada_w, ada_b):
    bsz = c.shape[0]
    n = ada_w.shape[1]
    return pl.pallas_call(
        _ada_kernel,
        grid=(n // D_MODEL,),
        in_specs=[
            pl.BlockSpec((bsz, D_MODEL), lambda j: (0, 0)),
            pl.BlockSpec((D_MODEL, D_MODEL), lambda j: (0, j)),
            pl.BlockSpec((1, D_MODEL), lambda j: (0, j)),
        ],
        out_specs=pl.BlockSpec((bsz, D_MODEL), lambda j: (0, j)),
        out_shape=jax.ShapeDtypeStruct((bsz, n), F32),
        name="ada",
    )(c, ada_w, ada_b.reshape(1, n))


def _proj_kernel(x_ref, mod_ref, g_ref, w_ref, wg_ref, o_ref, og_ref, h_ref):
    @pl.when(pl.program_id(1) == 0)
    def _():
        x = x_ref[...]
        y = x * lax.rsqrt(jnp.mean(x * x, axis=-1, keepdims=True) + EPS) * g_ref[...]
        m = mod_ref[0]
        hb = (y * (1.0 + m[1:2]) + m[0:1]).astype(BF16)
        h_ref[...] = hb
        og_ref[...] = _mm(hb, wg_ref[...])

    o_ref[...] = _mm(h_ref[...], w_ref[...]).astype(BF16)


def _proj(x2, mod3, norm_g, w_main, w_gate, seq):
    n_tok = x2.shape[0]
    n_main = w_main.shape[1]
    tm = min(ROW_TILE_PROJ, seq)
    tn = D_MODEL
    return pl.pallas_call(
        _proj_kernel,
        grid=(n_tok // tm, n_main // tn),
        in_specs=[
            pl.BlockSpec((tm, D_MODEL), lambda i, j: (i, 0)),
            pl.BlockSpec((1, N_MOD, D_MODEL), lambda i, j: ((i * tm) // seq, 0, 0)),
            pl.BlockSpec((1, D_MODEL), lambda i, j: (0, 0)),
            pl.BlockSpec((D_MODEL, tn), lambda i, j: (0, j)),
            pl.BlockSpec((D_MODEL, LANES), lambda i, j: (0, 0)),
        ],
        out_specs=[
            pl.BlockSpec((tm, tn), lambda i, j: (i, j)),
            pl.BlockSpec((tm, LANES), lambda i, j: (i, 0)),
        ],
        out_shape=[
            jax.ShapeDtypeStruct((n_tok, n_main), BF16),
            jax.ShapeDtypeStruct((n_tok, LANES), F32),
        ],
        scratch_shapes=[pltpu.VMEM((tm, D_MODEL), BF16)],
        compiler_params=pltpu.CompilerParams(
            dimension_semantics=("arbitrary", "arbitrary"), vmem_limit_bytes=VMEM_LIMIT),
        name="proj",
    )(x2, mod3, norm_g.reshape(1, D_MODEL), w_main, w_gate)


def _sb_kernel(q_ref, k_ref, v_ref, uu_ref, o_ref, qh_s, run_s, acc_s):
    seq = q_ref.shape[0]
    qt = QUERY_TILE
    tq, tk = min(SB_Q_SPAN, seq), min(SB_K_SPAN, seq)
    n_heads = LANES // SB_HEAD_DIM
    n_qb, n_kb = tq // qt, tk // qt
    lane = lax.broadcasted_iota(I32, (qt, LANES), 1)
    strict = lane < lax.broadcasted_iota(I32, (qt, LANES), 0)
    uu = uu_ref[...]

    def group(s0, diag):
        k = k_ref[pl.ds(s0, tk), :]
        v = v_ref[pl.ds(s0, tk), :]
        chains = range(n_qb * n_heads)
        nbs = [ci // n_heads + 1 if diag else n_kb for ci in chains]
        blk = lambda x, b: x[:, b * qt:(b + 1) * qt]
        zs = [_nt(qh_s[ci], k[:nbs[ci] * qt]) for ci in chains]
        lkb = []
        for ci in chains:
            z = zs[ci]
            nz = -z
            lk = jnp.minimum(nz, 0.0) - jnp.log(1.0 + jnp.exp2(jnp.minimum(z, nz))) * LOG2E
            lks = [blk(lk, b) for b in range(nbs[ci])]
            if diag:
                lks[-1] = jnp.where(strict, lks[-1], 0.0)
            lkb.append([x.astype(BF16) for x in lks])
        ccs = [[_mm(lkb[ci][b], uu) for b in range(nbs[ci])] for ci in chains]
        probs = []
        for ci in chains:
            run = run_s[ci]
            a = [None] * nbs[ci]
            for b in reversed(range(nbs[ci])):
                a[b] = jnp.exp2(blk(zs[ci], b) + ccs[ci][b][:, :LANES] + run)
                run = run + ccs[ci][b][:, LANES:]
            run_s[ci] = run
            if diag:
                a[-1] = jnp.where(strict, a[-1], 0.0)
            probs.append(jnp.concatenate(a, axis=1).astype(BF16))
        for ci in chains:
            acc_s[ci] = acc_s[ci] + _mm(probs[ci], v[:nbs[ci] * qt])

    def qspan(i, carry):
        r0 = pl.multiple_of(i * tq, tq)
        run_s[...] = jnp.zeros_like(run_s)
        acc_s[...] = jnp.zeros_like(acc_s)
        for qb in range(n_qb):
            qf = q_ref[pl.ds(r0 + qb * qt, qt), :].astype(F32) * (SB_HEAD_DIM ** -0.5 * LOG2E)
            for h in range(n_heads):
                in_head = (lane >= h * SB_HEAD_DIM) & (lane < (h + 1) * SB_HEAD_DIM)
                qh_s[qb * n_heads + h] = jnp.where(in_head, qf, 0.0).astype(BF16)

        group(r0, True)

        def past(j, c):
            group(pl.multiple_of((i - 1 - j) * tk, tk), False)
            return c

        lax.fori_loop(0, i, past, 0)
        for qb in range(n_qb):
            o_ref[pl.ds(r0 + qb * qt, qt), :] = jnp.where(
                lane < SB_HEAD_DIM, acc_s[qb * n_heads], acc_s[qb * n_heads + 1]).astype(BF16)
        return carry

    lax.fori_loop(0, seq // tq, qspan, 0)


def _sb_attention(proj, uu, bsz, seq):
    n_pairs = D_MODEL // LANES
    tq, tk = min(SB_Q_SPAN, seq), min(SB_K_SPAN, seq)
    assert tk == tq and seq % tk == 0 and LANES // SB_HEAD_DIM == 2
    n_chain = (tq // QUERY_TILE) * (LANES // SB_HEAD_DIM)
    return pl.pallas_call(
        _sb_kernel,
        grid=(bsz, n_pairs),
        in_specs=[
            pl.BlockSpec((seq, LANES), lambda b, p: (b, p)),
            pl.BlockSpec((seq, LANES), lambda b, p: (b, n_pairs + p)),
            pl.BlockSpec((seq, LANES), lambda b, p: (b, 2 * n_pairs + p)),
            pl.BlockSpec((LANES, 2 * LANES), lambda b, p: (0, 0)),
        ],
        out_specs=pl.BlockSpec((seq, LANES), lambda b, p: (b, p)),
        out_shape=jax.ShapeDtypeStruct((bsz * seq, D_MODEL), BF16),
        scratch_shapes=[
            pltpu.VMEM((n_chain, QUERY_TILE, LANES), BF16),
            pltpu.VMEM((n_chain, QUERY_TILE, LANES), F32),
            pltpu.VMEM((n_chain, QUERY_TILE, LANES), F32),
        ],
        compiler_params=pltpu.CompilerParams(dimension_semantics=("arbitrary", "arbitrary")),
        name="sb_attn",
    )(proj, proj, proj, uu)


def _mlstm_kernel(q_ref, k_ref, v_ref, og_ref, gt_ref, gb_ref, cwq_ref, cwk_ref, cbq_ref, cbk_ref, ng_ref, tri_ref,
                  y_ref, xq_s, xk_s, c_s, n_s, m_s):
    hd = ML_HEAD_DIM
    heads = range(ML_GROUP)
    first = pl.program_id(1) * ML_GROUP
    seq = q_ref.shape[0]
    cl = min(ML_CHUNK, seq)
    pad = SUBLANES
    for h in heads:
        xq_s[h, 0:pad, :] = jnp.zeros((pad, hd), F32)
        xk_s[h, 0:pad, :] = jnp.zeros((pad, hd), F32)
    c_s[...] = jnp.zeros_like(c_s)
    n_s[...] = jnp.zeros_like(n_s)
    m_s[...] = jnp.zeros_like(m_s)
    lane = lax.broadcasted_iota(I32, (cl, LANES), 1)
    sel_i = [(lane == first + h).astype(F32) for h in heads]
    sel_f = [(lane == first + h + ML_HEADS).astype(F32) for h in heads]
    tril = lax.broadcasted_iota(I32, (cl, cl), 1) <= lax.broadcasted_iota(I32, (cl, cl), 0)
    cols = lambda h: slice(h * hd, (h + 1) * hd)

    def conv_silu(x_s, h, cw_ref, cb_ref):
        y = cb_ref[:, cols(h)]
        for tap in range(CONV_WIDTH):
            lo = pad - (CONV_WIDTH - 1) + tap
            y = y + x_s[h, lo:lo + cl, :] * cw_ref[tap:tap + 1, cols(h)]
        x_s[h, 0:pad, :] = x_s[h, cl:cl + pad, :]
        return y * _sigmoid(y)

    def chunk(c, carry):
        rows = pl.ds(pl.multiple_of(c * cl, cl), cl)
        for h in heads:
            xq_s[h, pad:pad + cl, :] = q_ref[rows, cols(h)].astype(F32)
            xk_s[h, pad:pad + cl, :] = k_ref[rows, cols(h)].astype(F32)
        qc = [conv_silu(xq_s, h, cwq_ref, cbq_ref) for h in heads]
        kc = [conv_silu(xk_s, h, cwk_ref, cbk_ref) * (hd ** -0.5) for h in heads]
        qb = [x.astype(BF16) for x in qc]
        kb = [x.astype(BF16) for x in kc]
        vb = [v_ref[rows, cols(h)] for h in heads]

        pre = gt_ref[rows, :] + gb_ref[...]
        bt = _mm(tri_ref[...], _log_sigmoid(pre), precision=HIGHEST)
        b_col = [jnp.sum(bt * sel_f[h], axis=1, keepdims=True) for h in heads]
        i_col = [jnp.sum(pre * sel_i[h], axis=1, keepdims=True) for h in heads]
        g_col = [i_col[h] - b_col[h] for h in heads]
        g_row = [jnp.transpose(jnp.broadcast_to(g_col[h], (cl, LANES)))[0:1, :] for h in heads]

        m_prev = [m_s[h] for h in heads]
        dmat = [jnp.where(tril, b_col[h] + g_row[h], -jnp.inf) for h in heads]
        m_row = [jnp.maximum(b_col[h] + m_prev[h], jnp.max(dmat[h], axis=1, keepdims=True)) for h in heads]
        w_inter = [jnp.exp(b_col[h] + m_prev[h] - m_row[h]) for h in heads]
        qk = [_nt(qb[h], kb[h]) for h in heads]
        s_mat = [qk[h] * jnp.exp(dmat[h] - m_row[h]) for h in heads]
        q_c = [_mm(qb[h], c_s[h].astype(BF16)) for h in heads]
        s_v = [_mm(s_mat[h].astype(BF16), vb[h]) for h in heads]
        den = [w_inter[h] * jnp.sum(qc[h] * n_s[h], axis=1, keepdims=True) + jnp.sum(s_mat[h], axis=1, keepdims=True)
               for h in heads]
        hh = [(w_inter[h] * q_c[h] + s_v[h]) * (1.0 / jnp.maximum(jnp.abs(den[h]), jnp.exp(-m_row[h])))
              for h in heads]

        m_new = [m_row[h][cl - 1:cl, :] for h in heads]
        b_last = [b_col[h][cl - 1:cl, :] for h in heads]
        decay = [jnp.exp(b_last[h] + m_prev[h] - m_new[h]) for h in heads]
        wk = [jnp.exp(b_last[h] + g_col[h] - m_new[h]) * kc[h] for h in heads]
        k_v = [lax.dot_general(wk[h].astype(BF16), vb[h], (((0,), (0,)), ((), ())), preferred_element_type=F32)
               for h in heads]
        for h in heads:
            c_s[h] = decay[h] * c_s[h] + k_v[h]
            n_s[h] = decay[h] * n_s[h] + jnp.sum(wk[h], axis=0, keepdims=True)
            m_s[h] = m_new[h]
        for h in heads:
            hn = hh[h] * lax.rsqrt(jnp.mean(hh[h] * hh[h], axis=1, keepdims=True) + EPS) * ng_ref[:, cols(h)]
            y_ref[rows, cols(h)] = (hn * _sigmoid(og_ref[rows, cols(h)].astype(F32))).astype(BF16)
        return carry

    lax.fori_loop(0, seq // cl, chunk, 0)


def _mlstm(proj, gates, gate_bias, conv_w, conv_b, norm_g, tri, bsz, seq):
    hd = ML_HEAD_DIM
    g = ML_GROUP
    cl = min(ML_CHUNK, seq)
    n_grp = ML_HEADS // g
    col0 = 3 * D_MODEL // (g * hd)
    row_spec = lambda off: pl.BlockSpec((seq, g * hd), lambda b, j: (b, off + j))
    return pl.pallas_call(
        _mlstm_kernel,
        grid=(bsz, n_grp),
        in_specs=[
            row_spec(col0), row_spec(col0 + n_grp), row_spec(col0 + 2 * n_grp), row_spec(col0 + 3 * n_grp),
            pl.BlockSpec((seq, LANES), lambda b, j: (b, 0)),
            pl.BlockSpec((1, LANES), lambda b, j: (0, 0)),
            pl.BlockSpec((CONV_WIDTH, g * hd), lambda b, j: (0, j)),
            pl.BlockSpec((CONV_WIDTH, g * hd), lambda b, j: (0, n_grp + j)),
            pl.BlockSpec((1, g * hd), lambda b, j: (0, j)),
            pl.BlockSpec((1, g * hd), lambda b, j: (0, n_grp + j)),
            pl.BlockSpec((1, g * hd), lambda b, j: (0, j)),
            pl.BlockSpec((cl, cl), lambda b, j: (0, 0)),
        ],
        out_specs=pl.BlockSpec((seq, g * hd), lambda b, j: (b, j)),
        out_shape=jax.ShapeDtypeStruct((bsz * seq, D_MODEL), BF16),
        scratch_shapes=[
            pltpu.VMEM((g, cl + 2 * SUBLANES, hd), F32),
            pltpu.VMEM((g, cl + 2 * SUBLANES, hd), F32),
            pltpu.VMEM((g, hd, hd), F32),
            pltpu.VMEM((g, 1, hd), F32),
            pltpu.VMEM((g, 1, 1), F32),
        ],
        compiler_params=pltpu.CompilerParams(
            dimension_semantics=("arbitrary", "arbitrary"), vmem_limit_bytes=VMEM_LIMIT),
        name="mlstm",
    )(proj, proj, proj, proj, gates, gate_bias, conv_w, conv_w,
      conv_b.reshape(1, -1), conv_b.reshape(1, -1), norm_g.reshape(1, -1), tri)


def _merge_kernel(ya_ref, yb_ref, ga_ref, gb_ref, x_ref, mod_ref, wa_ref, wb_ref, wo_ref, n2_ref, rwt_ref, rb_ref,
                  su_ref, x1_ref, h2_ref, idx_ref, wt_ref, rank_ref, cnt_ref, carry_s):
    @pl.when(pl.program_id(0) == 0)
    def _():
        carry_s[...] = jnp.zeros_like(carry_s)

    ts = x_ref.shape[0] // MERGE_SUB
    subs = range(MERGE_SUB)
    rows = [slice(s * ts, (s + 1) * ts) for s in subs]
    m = mod_ref[0]
    logit = []
    for r in rows:
        a = _mm(ya_ref[r, :], wa_ref[...])
        b = _mm(yb_ref[r, :], wb_ref[...])
        merged = _sigmoid(ga_ref[r, :].astype(F32)) * a + _sigmoid(gb_ref[r, :].astype(F32)) * b
        x1 = x_ref[r, :] + m[2:3] * _mm(merged.astype(BF16), wo_ref[...])
        x1_ref[r, :] = x1
        y = x1 * lax.rsqrt(jnp.mean(x1 * x1, axis=-1, keepdims=True) + EPS) * n2_ref[...]
        h2 = y * (1.0 + m[4:5]) + m[3:4]
        h2_ref[r, :] = h2
        logit.append(_nt(rwt_ref[...], h2, precision=HIGHEST) + rb_ref[...])

    ie = lax.broadcasted_iota(I32, (N_EXPERTS, ts), 0).astype(F32)
    carry = carry_s[:, 0:1]
    wts, idx_out, rank_out = [], [], []
    for s in subs:
        logits = logit[s]
        idxs, vals = [], []
        for _ in range(TOP_K):
            mx = jnp.max(logits, axis=0, keepdims=True)
            am = jnp.min(jnp.where(logits == mx, ie, float(N_EXPERTS)), axis=0, keepdims=True)
            idxs.append(am)
            vals.append(mx)
            logits = jnp.where(ie == am, -jnp.inf, logits)
        exps = [jnp.exp(v - vals[0]) for v in vals]
        inv = 1.0 / (exps[0] + exps[1] + exps[2] + exps[3])
        wts.append(jnp.concatenate([e * inv for e in exps], axis=0))
        idx_out.append(jnp.concatenate(idxs, axis=0).astype(I32))

        onehot = jnp.zeros((N_EXPERTS, ts), F32)
        for am in idxs:
            onehot = onehot + (ie == am).astype(F32)
        before = _mm(onehot.astype(BF16), su_ref[...]) + carry
        ranks = [jnp.sum(jnp.where(ie == am, before, 0.0), axis=0, keepdims=True) for am in idxs]
        rank_out.append(jnp.concatenate(ranks, axis=0).astype(I32))
        carry = carry + jnp.sum(onehot, axis=1, keepdims=True)
    wt_ref[...] = jnp.concatenate(wts, axis=1)
    idx_ref[...] = jnp.concatenate(idx_out, axis=1)
    rank_ref[...] = jnp.concatenate(rank_out, axis=1)
    carry_s[...] = jnp.broadcast_to(carry, carry_s.shape)
    cnt_ref[...] = carry_s[...]


def _merge(ya, yb, proj, x2, mod3, w_a, w_b, w_o, norm2_g, router_wt, router_b, su, seq):
    n_tok = x2.shape[0]
    tm = min(ROW_TILE_MERGE, seq)
    d = D_MODEL
    gcol = 7 * d // d
    row = lambda j: pl.BlockSpec((tm, d), lambda i: (i, j))
    const = lambda shape: pl.BlockSpec(shape, lambda i: tuple(0 for _ in shape))
    tok4 = pl.BlockSpec((TOP_K, tm), lambda i: (0, i))
    return pl.pallas_call(
        _merge_kernel,
        grid=(n_tok // tm,),
        in_specs=[
            row(0), row(0), row(gcol), row(gcol + 1), row(0),
            pl.BlockSpec((1, N_MOD, d), lambda i: ((i * tm) // seq, 0, 0)),
            const((d, d)), const((d, d)), const((d, d)), const((1, d)),
            const((N_EXPERTS, d)), const((N_EXPERTS, 1)), const((tm // MERGE_SUB, tm // MERGE_SUB)),
        ],
        out_specs=[
            row(0), row(0),
            tok4, tok4, tok4,
            const((N_EXPERTS, LANES)),
        ],
        out_shape=[
            jax.ShapeDtypeStruct((n_tok, d), F32),
            jax.ShapeDtypeStruct((n_tok, d), F32),
            jax.ShapeDtypeStruct((TOP_K, n_tok), I32),
            jax.ShapeDtypeStruct((TOP_K, n_tok), F32),
            jax.ShapeDtypeStruct((TOP_K, n_tok), I32),
            jax.ShapeDtypeStruct((N_EXPERTS, LANES), F32),
        ],
        scratch_shapes=[pltpu.VMEM((N_EXPERTS, LANES), F32)],
        compiler_params=pltpu.CompilerParams(dimension_semantics=("arbitrary",), vmem_limit_bytes=VMEM_LIMIT),
        name="merge",
    )(ya, yb, proj, proj, x2, mod3, w_a, w_b, w_o, norm2_g.reshape(1, d), router_wt, router_b.reshape(-1, 1), su)


def _scatter_kernel(start_ref, end_ref, nb_ref, dest_ref, h2_ref, xs_ref, zero_s, sem, zsem):
    tm = h2_ref.shape[0]
    bm = zero_s.shape[0]

    @pl.when(pl.program_id(0) == 0)
    def _():
        zero_s[...] = jnp.zeros_like(zero_s)

        def zero_block(r0):
            pltpu.make_async_copy(zero_s, xs_ref.at[pl.ds(pl.multiple_of(r0, bm), bm), :], zsem).start()

        def per_expert(e, n):
            lo, hi = start_ref[e], end_ref[e]

            @pl.when(hi > lo)
            def _():
                zero_block(hi - bm)

            return n + (hi > lo).astype(I32)

        n_fill = lax.fori_loop(0, N_EXPERTS, per_expert, 0)
        n_blocks = xs_ref.shape[0] // bm

        def fill_block(j, c):
            zero_block(j * bm)
            return c

        lax.fori_loop(nb_ref[0], n_blocks, fill_block, 0)

        def drain_block(j, c):
            pltpu.make_async_copy(zero_s, xs_ref.at[pl.ds(0, bm), :], zsem).wait()
            return c

        lax.fori_loop(0, n_fill + n_blocks - nb_ref[0], drain_block, 0)

    def send(g, c):
        for u in range(DMA_UNROLL):
            t = g * DMA_UNROLL + u
            for r in range(TOP_K):
                pltpu.make_async_copy(
                    h2_ref.at[pl.ds(t, 1), :], xs_ref.at[pl.ds(dest_ref[r, t], 1), :], sem).start()
        return c

    lax.fori_loop(0, tm // DMA_UNROLL, send, 0)

    def drain(g, c):
        for _ in range(DMA_UNROLL * TOP_K):
            pltpu.make_async_copy(h2_ref.at[pl.ds(0, 1), :], xs_ref.at[pl.ds(0, 1), :], sem).wait()
        return c

    lax.fori_loop(0, tm // DMA_UNROLL, drain, 0)


def _scatter(start, end, n_used, dest, h2s, n_rows, seq):
    n_tok = h2s.shape[0]
    tm = min(MOE_TOKEN_TILE, seq)
    smem = lambda: pl.BlockSpec(memory_space=pltpu.SMEM)
    tok4 = pl.BlockSpec((TOP_K, tm), lambda i: (0, i), memory_space=pltpu.SMEM)
    return pl.pallas_call(
        _scatter_kernel,
        grid=(n_tok // tm,),
        in_specs=[smem(), smem(), smem(), tok4, pl.BlockSpec((tm, D_MODEL), lambda i: (i, 0))],
        out_specs=pl.BlockSpec(memory_space=pl.ANY),
        out_shape=jax.ShapeDtypeStruct((n_rows, D_MODEL), F32),
        scratch_shapes=[pltpu.VMEM((ROW_TILE_MOE, D_MODEL), F32), pltpu.SemaphoreType.DMA,
                        pltpu.SemaphoreType.DMA],
        compiler_params=pltpu.CompilerParams(dimension_semantics=("arbitrary",)),
        name="scatter",
    )(start, end, n_used, dest, h2s)


def _expert_kernel(be_ref, nb_ref, xs_ref, w1_ref, b1_ref, w2_ref, b2_ref, ys_ref, w1_s, w2_s):
    j = pl.program_id(0)
    changed = jnp.logical_or(j == 0, be_ref[j] != be_ref[jnp.maximum(j - 1, 0)])

    @pl.when(changed)
    def _():
        w1_s[...] = w1_ref[...].astype(BF16)
        w2_s[...] = w2_ref[...].astype(BF16)

    @pl.when(j < nb_ref[0])
    def _():
        a = _mm(xs_ref[...].astype(BF16), w1_s[...]) + b1_ref[...]
        glu = jnp.minimum(a[:, :D_FF], SWIGLU_LIMIT)
        lin = jnp.clip(a[:, D_FF:], -SWIGLU_LIMIT, SWIGLU_LIMIT)
        hmid = (lin + 1.0) * (glu * _sigmoid(SWIGLU_ALPHA * glu))
        ys_ref[...] = _mm(hmid.astype(BF16), w2_s[...]) + b2_ref[...]

    @pl.when(j >= nb_ref[0])
    def _():
        ys_ref[...] = jnp.zeros_like(ys_ref)


def _experts(block_e, n_blocks_used, xs, w1, b1, w2, b2):
    n_rows = xs.shape[0]
    bm = ROW_TILE_MOE
    d = D_MODEL
    blk = lambda j, be, nb: (j, 0)
    exp = lambda j, be, nb: (be[j], 0, 0)
    return pl.pallas_call(
        _expert_kernel,
        grid_spec=pltpu.PrefetchScalarGridSpec(
            num_scalar_prefetch=2,
            grid=(n_rows // bm,),
            in_specs=[
                pl.BlockSpec((bm, d), blk),
                pl.BlockSpec((None, d, 2 * D_FF), exp),
                pl.BlockSpec((None, 1, 2 * D_FF), exp),
                pl.BlockSpec((None, D_FF, d), exp),
                pl.BlockSpec((None, 1, d), exp),
            ],
            out_specs=pl.BlockSpec((bm, d), blk),
            scratch_shapes=[pltpu.VMEM((d, 2 * D_FF), BF16), pltpu.VMEM((D_FF, d), BF16)],
        ),
        out_shape=jax.ShapeDtypeStruct((n_rows, d), F32),
        compiler_params=pltpu.CompilerParams(dimension_semantics=("arbitrary",), vmem_limit_bytes=VMEM_LIMIT),
        name="experts",
    )(block_e, n_blocks_used, xs, w1, b1.reshape(N_EXPERTS, 1, -1), w2, b2.reshape(N_EXPERTS, 1, -1))


def _combine_kernel(dest_ref, wt_ref, ys_ref, x1_ref, mod_ref, fg_ref, o_ref, buf_s, sem):
    tm = x1_ref.shape[0]

    def fetch(g, c):
        for u in range(DMA_UNROLL):
            t = g * DMA_UNROLL + u
            for r in range(TOP_K):
                pltpu.make_async_copy(
                    ys_ref.at[pl.ds(dest_ref[r, t], 1), :], buf_s.at[r, g, pl.ds(u, 1), :], sem).start()
        return c

    lax.fori_loop(0, tm // DMA_UNROLL, fetch, 0)

    def drain(g, c):
        for _ in range(DMA_UNROLL * TOP_K):
            pltpu.make_async_copy(ys_ref.at[pl.ds(0, 1), :], buf_s.at[0, 0, pl.ds(0, 1), :], sem).wait()
        return c

    lax.fori_loop(0, tm // DMA_UNROLL, drain, 0)

    wt = wt_ref[...]
    rows = lambda r: buf_s[r].reshape(tm, D_MODEL)
    moe = wt[:, 0:1] * rows(0)
    for r in range(1, TOP_K):
        moe = moe + wt[:, r:r + 1] * rows(r)
    x2 = x1_ref[...] + mod_ref[0][5:6] * moe
    o_ref[...] = x2 * lax.rsqrt(jnp.mean(x2 * x2, axis=-1, keepdims=True) + EPS) * fg_ref[...]


def _combine(dest, wt, ys, x1, mod3, final_g, seq):
    assert DMA_UNROLL == SUBLANES
    n_tok = x1.shape[0]
    tm = min(MOE_TOKEN_TILE, seq)
    d = D_MODEL
    tok4 = pl.BlockSpec((TOP_K, tm), lambda i: (0, i), memory_space=pltpu.SMEM)
    return pl.pallas_call(
        _combine_kernel,
        grid=(n_tok // tm,),
        in_specs=[
            tok4,
            pl.BlockSpec((tm, TOP_K), lambda i: (i, 0)),
            pl.BlockSpec(memory_space=pl.ANY),
            pl.BlockSpec((tm, d), lambda i: (i, 0)),
            pl.BlockSpec((1, N_MOD, d), lambda i: ((i * tm) // seq, 0, 0)),
            pl.BlockSpec((1, d), lambda i: (0, 0)),
        ],
        out_specs=pl.BlockSpec((tm, d), lambda i: (i, 0)),
        out_shape=jax.ShapeDtypeStruct((n_tok, d), F32),
        scratch_shapes=[pltpu.VMEM((TOP_K, tm // DMA_UNROLL, DMA_UNROLL, d), F32), pltpu.SemaphoreType.DMA],
        compiler_params=pltpu.CompilerParams(dimension_semantics=("arbitrary",), vmem_limit_bytes=VMEM_LIMIT),
        name="combine",
    )(dest, wt.T, ys, x1, mod3, final_g.reshape(1, d))


def _layer(x2, mod3, bsz, seq, norm1_g, w_in, conv_w, conv_b, ml_b_i, ml_b_f, ml_norm_g, w_branch_a, w_branch_b,
           w_out, norm2_g, router_w, router_b, expert_w1, expert_b1, expert_w2, expert_b2):
    d = D_MODEL
    n_tok = bsz * seq
    gate0 = 7 * d
    n_gate = 2 * ML_HEADS
    w_main = jnp.concatenate([w_in[:, :gate0], w_in[:, gate0 + n_gate:]], axis=1).astype(BF16)
    w_gate = jnp.pad(w_in[:, gate0:gate0 + n_gate], ((0, 0), (0, LANES - n_gate))).astype(BF16)
    gate_bias = jnp.pad(jnp.concatenate([ml_b_i, ml_b_f]), (0, LANES - n_gate)).reshape(1, LANES)

    proj, gates = _proj(x2, mod3, norm1_g, w_main, w_gate, seq)

    ones = jnp.ones((LANES, LANES), F32)
    uu = jnp.concatenate([jnp.tril(ones), ones], axis=1).astype(BF16)
    ya = _sb_attention(proj, uu, bsz, seq)

    cl = min(ML_CHUNK, seq)
    tri = jnp.tril(jnp.ones((cl, cl), F32))
    yb = _mlstm(proj, gates, gate_bias, conv_w, conv_b, ml_norm_g, tri, bsz, seq)

    ts = min(ROW_TILE_MERGE, seq) // MERGE_SUB
    su = jnp.triu(jnp.ones((ts, ts), F32), k=1).astype(BF16)
    x1, h2s, idx, wt, rank, cnt = _merge(
        ya, yb, proj, x2, mod3, w_branch_a.astype(BF16), w_branch_b.astype(BF16), w_out.astype(BF16),
        norm2_g, router_w.T, router_b, su, seq)

    bm = ROW_TILE_MOE
    counts = cnt[:, 0].astype(I32)
    padded = (counts + bm - 1) // bm * bm
    pad_end = jnp.cumsum(padded)
    start = pad_end - padded
    n_rows = (n_tok * TOP_K + N_EXPERTS * (bm - 1)) // bm * bm
    n_blocks = n_rows // bm
    n_used = jnp.maximum(pad_end[-1] // bm, 1).astype(I32)
    blk = jnp.minimum(jnp.arange(n_blocks, dtype=I32), n_used - 1)
    block_e = jnp.minimum(jnp.sum((pad_end[None, :] <= (blk * bm)[:, None]).astype(I32), axis=1), N_EXPERTS - 1)

    onehot = idx[:, :, None] == jnp.arange(N_EXPERTS, dtype=I32)
    dest = rank + jnp.sum(jnp.where(onehot, start, 0), axis=-1)

    xs = _scatter(start, pad_end, n_used.reshape(1), dest, h2s, n_rows, seq)
    ys = _experts(block_e, n_used.reshape(1), xs, expert_w1, expert_b1, expert_w2, expert_b2)
    return dest, wt, ys, x1


def kernel(x, c, ada_w, ada_b, norm1_g, w_in, conv_w, conv_b, ml_b_i, ml_b_f, ml_norm_g, w_branch_a, w_branch_b, w_out, norm2_g, router_w, router_b, expert_w1, expert_b1, expert_w2, expert_b2, final_g):
    bsz, seq, d = x.shape
    depth = ada_w.shape[0]
    assert d == D_MODEL and depth == 1 and seq % QUERY_TILE == 0
    x2 = x.reshape(bsz * seq, d)
    mod3 = _ada(c, ada_w[0], ada_b[0]).reshape(bsz, N_MOD, d)
    dest, wt, ys, x1 = _layer(
        x2, mod3, bsz, seq, norm1_g[0], w_in[0], conv_w[0], conv_b[0], ml_b_i[0], ml_b_f[0], ml_norm_g[0],
        w_branch_a[0], w_branch_b[0], w_out[0], norm2_g[0], router_w[0], router_b[0],
        expert_w1[0], expert_b1[0], expert_w2[0], expert_b2[0])
    out = _combine(dest, wt, ys, x1, mod3, final_g, seq)
    return out.reshape(bsz, seq, d)
```

```python
import functools

import jax
import jax.numpy as jnp
from jax import lax
from jax.experimental import pallas as pl
from jax.experimental.pallas import tpu as pltpu

F32 = jnp.float32
BF16 = jnp.bfloat16
I32 = jnp.int32
HIGHEST = lax.Precision.HIGHEST

D_MODEL = 1024
SB_HEAD_DIM = 64
ML_HEADS = 4
ML_HEAD_DIM = 256
CONV_WIDTH = 4
N_EXPERTS = 32
TOP_K = 4
D_FF = 1024
SWIGLU_LIMIT = 7.0
SWIGLU_ALPHA = 1.702
N_MOD = 6
EPS = 1e-6
LOG2E = 1.4426950408889634

LANES = 128
SUBLANES = 8
DMA_UNROLL = 8

QUERY_TILE = 128
SB_Q_SPAN = 512
SB_K_SPAN = 512
ML_CHUNK = 256
ML_GROUP = 4
ROW_TILE_PROJ = 1024
ROW_TILE_MERGE = 512
MERGE_SUB = 1
ROW_TILE_MOE = 512
MOE_TOKEN_TILE = 512
VMEM_LIMIT = 56 * 1024 * 1024


def _nt(a, b, precision=None):
    return lax.dot_general(a, b, (((1,), (1,)), ((), ())), preferred_element_type=F32, precision=precision)


def _mm(a, b, precision=None):
    return jnp.dot(a, b, preferred_element_type=F32, precision=precision)


def _sigmoid(x):
    return 1.0 / (1.0 + jnp.exp(-x))


def _log_sigmoid(x):
    return jnp.minimum(x, 0.0) - jnp.log(1.0 + jnp.exp(-jnp.abs(x)))


def _ada_kernel(c_ref, w_ref, b_ref, o_ref):
    c = c_ref[...]
    o_ref[...] = _mm(c * _sigmoid(c), w_ref[...], precision=HIGHEST) + b_ref[...]


def _ada(c, ada_w, ada_b):
    bsz = c.shape[0]
    n = ada_w.shape[1]
    return pl.pallas_call(
        _ada_kernel,
        grid=(n // D_MODEL,),
        in_specs=[
            pl.BlockSpec((bsz, D_MODEL), lambda j: (0, 0)),
            pl.BlockSpec((D_MODEL, D_MODEL), lambda j: (0, j)),
            pl.BlockSpec((1, D_MODEL), lambda j: (0, j)),
        ],
        out_specs=pl.BlockSpec((bsz, D_MODEL), lambda j: (0, j)),
        out_shape=jax.ShapeDtypeStruct((bsz, n), F32),
        name="ada",
    )(c, ada_w, ada_b.reshape(1, n))


def _proj_kernel(x_ref, mod_ref, g_ref, w_ref, wg_ref, o_ref, og_ref, h_ref):
    @pl.when(pl.program_id(1) == 0)
    def _():
        x = x_ref[...]
        y = x * lax.rsqrt(jnp.mean(x * x, axis=-1, keepdims=True) + EPS) * g_ref[...]
        m = mod_ref[0]
        hb = (y * (1.0 + m[1:2]) + m[0:1]).astype(BF16)
        h_ref[...] = hb
        og_ref[...] = _mm(hb, wg_ref[...])

    o_ref[...] = _mm(h_ref[...], w_ref[...]).astype(BF16)


def _proj(x2, mod3, norm_g, w_main, w_gate, seq):
    n_tok = x2.shape[0]
    n_main = w_main.shape[1]
    tm = min(ROW_TILE_PROJ, seq)
    tn = D_MODEL
    return pl.pallas_call(
        _proj_kernel,
        grid=(n_tok // tm, n_main // tn),
        in_specs=[
            pl.BlockSpec((tm, D_MODEL), lambda i, j: (i, 0)),
            pl.BlockSpec((1, N_MOD, D_MODEL), lambda i, j: ((i * tm) // seq, 0, 0)),
            pl.BlockSpec((1, D_MODEL), lambda i, j: (0, 0)),
            pl.BlockSpec((D_MODEL, tn), lambda i, j: (0, j)),
            pl.BlockSpec((D_MODEL, LANES), lambda i, j: (0, 0)),
        ],
        out_specs=[
            pl.BlockSpec((tm, tn), lambda i, j: (i, j)),
            pl.BlockSpec((tm, LANES), lambda i, j: (i, 0)),
        ],
        out_shape=[
            jax.ShapeDtypeStruct((n_tok, n_main), BF16),
            jax.ShapeDtypeStruct((n_tok, LANES), F32),
        ],
        scratch_shapes=[pltpu.VMEM((tm, D_MODEL), BF16)],
        compiler_params=pltpu.CompilerParams(
            dimension_semantics=("arbitrary", "arbitrary"), vmem_limit_bytes=VMEM_LIMIT),
        name="proj",
    )(x2, mod3, norm_g.reshape(1, D_MODEL), w_main, w_gate)


def _sb_kernel(q_ref, k_ref, v_ref, uu_ref, o_ref, qh_s, run_s, acc_s):
    seq = q_ref.shape[0]
    qt = QUERY_TILE
    tq, tk = min(SB_Q_SPAN, seq), min(SB_K_SPAN, seq)
    n_heads = LANES // SB_HEAD_DIM
    n_qb, n_kb = tq // qt, tk // qt
    lane = lax.broadcasted_iota(I32, (qt, LANES), 1)
    strict = lane < lax.broadcasted_iota(I32, (qt, LANES), 0)
    uu = uu_ref[...]

    def group(s0, diag):
        k = k_ref[pl.ds(s0, tk), :]
        v = v_ref[pl.ds(s0, tk), :]
        chains = range(n_qb * n_heads)
        nbs = [ci // n_heads + 1 if diag else n_kb for ci in chains]
        blk = lambda x, b: x[:, b * qt:(b + 1) * qt]
        zs = [_nt(qh_s[ci], k[:nbs[ci] * qt]) for ci in chains]
        lkb = []
        for ci in chains:
            z = zs[ci]
            nz = -z
            lk = jnp.minimum(nz, 0.0) - jnp.log(1.0 + jnp.exp2(jnp.minimum(z, nz))) * LOG2E
            lks = [blk(lk, b) for b in range(nbs[ci])]
            if diag:
                lks[-1] = jnp.where(strict, lks[-1], 0.0)
            lkb.append([x.astype(BF16) for x in lks])
        ccs = [[_mm(lkb[ci][b], uu) for b in range(nbs[ci])] for ci in chains]
        probs = []
        for ci in chains:
            run = run_s[ci]
            a = [None] * nbs[ci]
            for b in reversed(range(nbs[ci])):
                a[b] = jnp.exp2(blk(zs[ci], b) + ccs[ci][b][:, :LANES] + run)
                run = run + ccs[ci][b][:, LANES:]
            run_s[ci] = run
            if diag:
                a[-1] = jnp.where(strict, a[-1], 0.0)
            probs.append(jnp.concatenate(a, axis=1).astype(BF16))
        for ci in chains:
            acc_s[ci] = acc_s[ci] + _mm(probs[ci], v[:nbs[ci] * qt])

    def qspan(i, carry):
        r0 = pl.multiple_of(i * tq, tq)
        run_s[...] = jnp.zeros_like(run_s)
        acc_s[...] = jnp.zeros_like(acc_s)
        for qb in range(n_qb):
            qf = q_ref[pl.ds(r0 + qb * qt, qt), :].astype(F32) * (SB_HEAD_DIM ** -0.5 * LOG2E)
            for h in range(n_heads):
                in_head = (lane >= h * SB_HEAD_DIM) & (lane < (h + 1) * SB_HEAD_DIM)
                qh_s[qb * n_heads + h] = jnp.where(in_head, qf, 0.0).astype(BF16)

        group(r0, True)

        def past(j, c):
            group(pl.multiple_of((i - 1 - j) * tk, tk), False)
            return c

        lax.fori_loop(0, i, past, 0)
        for qb in range(n_qb):
            o_ref[pl.ds(r0 + qb * qt, qt), :] = jnp.where(
                lane < SB_HEAD_DIM, acc_s[qb * n_heads], acc_s[qb * n_heads + 1]).astype(BF16)
        return carry

    lax.fori_loop(0, seq // tq, qspan, 0)


def _sb_attention(proj, uu, bsz, seq):
    n_pairs = D_MODEL // LANES
    tq, tk = min(SB_Q_SPAN, seq), min(SB_K_SPAN, seq)
    assert tk == tq and seq % tk == 0 and LANES // SB_HEAD_DIM == 2
    n_chain = (tq // QUERY_TILE) * (LANES // SB_HEAD_DIM)
    return pl.pallas_call(
        _sb_kernel,
        grid=(bsz, n_pairs),
        in_specs=[
            pl.BlockSpec((seq, LANES), lambda b, p: (b, p)),
            pl.BlockSpec((seq, LANES), lambda b, p: (b, n_pairs + p)),
            pl.BlockSpec((seq, LANES), lambda b, p: (b, 2 * n_pairs + p)),
            pl.BlockSpec((LANES, 2 * LANES), lambda b, p: (0, 0)),
        ],
        out_specs=pl.BlockSpec((seq, LANES), lambda b, p: (b, p)),
        out_shape=jax.ShapeDtypeStruct((bsz * seq, D_MODEL), BF16),
        scratch_shapes=[
            pltpu.VMEM((n_chain, QUERY_TILE, LANES), BF16),
            pltpu.VMEM((n_chain, QUERY_TILE, LANES), F32),
            pltpu.VMEM((n_chain, QUERY_TILE, LANES), F32),
        ],
        compiler_params=pltpu.CompilerParams(dimension_semantics=("arbitrary", "arbitrary")),
        name="sb_attn",
    )(proj, proj, proj, uu)


def _mlstm_kernel(q_ref, k_ref, v_ref, og_ref, gt_ref, gb_ref, cwq_ref, cwk_ref, cbq_ref, cbk_ref, ng_ref, tri_ref,
                  y_ref, xq_s, xk_s, c_s, n_s, m_s):
    hd = ML_HEAD_DIM
    heads = range(ML_GROUP)
    first = pl.program_id(1) * ML_GROUP
    seq = q_ref.shape[0]
    cl = min(ML_CHUNK, seq)
    pad = SUBLANES
    for h in heads:
        xq_s[h, 0:pad, :] = jnp.zeros((pad, hd), F32)
        xk_s[h, 0:pad, :] = jnp.zeros((pad, hd), F32)
    c_s[...] = jnp.zeros_like(c_s)
    n_s[...] = jnp.zeros_like(n_s)
    m_s[...] = jnp.zeros_like(m_s)
    lane = lax.broadcasted_iota(I32, (cl, LANES), 1)
    sel_i = [(lane == first + h).astype(F32) for h in heads]
    sel_f = [(lane == first + h + ML_HEADS).astype(F32) for h in heads]
    tril = lax.broadcasted_iota(I32, (cl, cl), 1) <= lax.broadcasted_iota(I32, (cl, cl), 0)
    cols = lambda h: slice(h * hd, (h + 1) * hd)

    def conv_silu(x_s, h, cw_ref, cb_ref):
        y = cb_ref[:, cols(h)]
        for tap in range(CONV_WIDTH):
            lo = pad - (CONV_WIDTH - 1) + tap
            y = y + x_s[h, lo:lo + cl, :] * cw_ref[tap:tap + 1, cols(h)]
        x_s[h, 0:pad, :] = x_s[h, cl:cl + pad, :]
        return y * _sigmoid(y)

    def chunk(c, carry):
        rows = pl.ds(pl.multiple_of(c * cl, cl), cl)
        for h in heads:
            xq_s[h, pad:pad + cl, :] = q_ref[rows, cols(h)].astype(F32)
            xk_s[h, pad:pad + cl, :] = k_ref[rows, cols(h)].astype(F32)
        qc = [conv_silu(xq_s, h, cwq_ref, cbq_ref) for h in heads]
        kc = [conv_silu(xk_s, h, cwk_ref, cbk_ref) * (hd ** -0.5) for h in heads]
        qb = [x.astype(BF16) for x in qc]
        kb = [x.astype(BF16) for x in kc]
        vb = [v_ref[rows, cols(h)] for h in heads]

        pre = gt_ref[rows, :] + gb_ref[...]
        bt = _mm(tri_ref[...], _log_sigmoid(pre), precision=HIGHEST)
        b_col = [jnp.sum(bt * sel_f[h], axis=1, keepdims=True) for h in heads]
        i_col = [jnp.sum(pre * sel_i[h], axis=1, keepdims=True) for h in heads]
        g_col = [i_col[h] - b_col[h] for h in heads]
        g_row = [jnp.transpose(jnp.broadcast_to(g_col[h], (cl, LANES)))[0:1, :] for h in heads]

        m_prev = [m_s[h] for h in heads]
        dmat = [jnp.where(tril, b_col[h] + g_row[h], -jnp.inf) for h in heads]
        m_row = [jnp.maximum(b_col[h] + m_prev[h], jnp.max(dmat[h], axis=1, keepdims=True)) for h in heads]
        w_inter = [jnp.exp(b_col[h] + m_prev[h] - m_row[h]) for h in heads]
        qk = [_nt(qb[h], kb[h]) for h in heads]
        s_mat = [qk[h] * jnp.exp(dmat[h] - m_row[h]) for h in heads]
        q_c = [_mm(qb[h], c_s[h].astype(BF16)) for h in heads]
        s_v = [_mm(s_mat[h].astype(BF16), vb[h]) for h in heads]
        den = [w_inter[h] * jnp.sum(qc[h] * n_s[h], axis=1, keepdims=True) + jnp.sum(s_mat[h], axis=1, keepdims=True)
               for h in heads]
        hh = [(w_inter[h] * q_c[h] + s_v[h]) * (1.0 / jnp.maximum(jnp.abs(den[h]), jnp.exp(-m_row[h])))
              for h in heads]

        m_new = [m_row[h][cl - 1:cl, :] for h in heads]
        b_last = [b_col[h][cl - 1:cl, :] for h in heads]
        decay = [jnp.exp(b_last[h] + m_prev[h] - m_new[h]) for h in heads]
        wk = [jnp.exp(b_last[h] + g_col[h] - m_new[h]) * kc[h] for h in heads]
        k_v = [lax.dot_general(wk[h].astype(BF16), vb[h], (((0,), (0,)), ((), ())), preferred_element_type=F32)
               for h in heads]
        for h in heads:
            c_s[h] = decay[h] * c_s[h] + k_v[h]
            n_s[h] = decay[h] * n_s[h] + jnp.sum(wk[h], axis=0, keepdims=True)
            m_s[h] = m_new[h]
        for h in heads:
            hn = hh[h] * lax.rsqrt(jnp.mean(hh[h] * hh[h], axis=1, keepdims=True) + EPS) * ng_ref[:, cols(h)]
            y_ref[rows, cols(h)] = (hn * _sigmoid(og_ref[rows, cols(h)].astype(F32))).astype(BF16)
        return carry

    lax.fori_loop(0, seq // cl, chunk, 0)


def _mlstm(proj, gates, gate_bias, conv_w, conv_b, norm_g, tri, bsz, seq):
    hd = ML_HEAD_DIM
    g = ML_GROUP
    cl = min(ML_CHUNK, seq)
    n_grp = ML_HEADS // g
    col0 = 3 * D_MODEL // (g * hd)
    row_spec = lambda off: pl.BlockSpec((seq, g * hd), lambda b, j: (b, off + j))
    return pl.pallas_call(
        _mlstm_kernel,
        grid=(bsz, n_grp),
        in_specs=[
            row_spec(col0), row_spec(col0 + n_grp), row_spec(col0 + 2 * n_grp), row_spec(col0 + 3 * n_grp),
            pl.BlockSpec((seq, LANES), lambda b, j: (b, 0)),
            pl.BlockSpec((1, LANES), lambda b, j: (0, 0)),
            pl.BlockSpec((CONV_WIDTH, g * hd), lambda b, j: (0, j)),
            pl.BlockSpec((CONV_WIDTH, g * hd), lambda b, j: (0, n_grp + j)),
            pl.BlockSpec((1, g * hd), lambda b, j: (0, j)),
            pl.BlockSpec((1, g * hd), lambda b, j: (0, n_grp + j)),
            pl.BlockSpec((1, g * hd), lambda b, j: (0, j)),
            pl.BlockSpec((cl, cl), lambda b, j: (0, 0)),
        ],
        out_specs=pl.BlockSpec((seq, g * hd), lambda b, j: (b, j)),
        out_shape=jax.ShapeDtypeStruct((bsz * seq, D_MODEL), BF16),
        scratch_shapes=[
            pltpu.VMEM((g, cl + 2 * SUBLANES, hd), F32),
            pltpu.VMEM((g, cl + 2 * SUBLANES, hd), F32),
            pltpu.VMEM((g, hd, hd), F32),
            pltpu.VMEM((g, 1, hd), F32),
            pltpu.VMEM((g, 1, 1), F32),
        ],
        compiler_params=pltpu.CompilerParams(
            dimension_semantics=("arbitrary", "arbitrary"), vmem_limit_bytes=VMEM_LIMIT),
        name="mlstm",
    )(proj, proj, proj, proj, gates, gate_bias, conv_w, conv_w,
      conv_b.reshape(1, -1), conv_b.reshape(1, -1), norm_g.reshape(1, -1), tri)


def _merge_kernel(ya_ref, yb_ref, ga_ref, gb_ref, x_ref, mod_ref, wa_ref, wb_ref, wo_ref, n2_ref, rwt_ref, rb_ref,
                  su_ref, x1_ref, h2_ref, idx_ref, wt_ref, rank_ref, cnt_ref, carry_s):
    @pl.when(pl.program_id(0) == 0)
    def _():
        carry_s[...] = jnp.zeros_like(carry_s)

    ts = x_ref.shape[0] // MERGE_SUB
    subs = range(MERGE_SUB)
    rows = [slice(s * ts, (s + 1) * ts) for s in subs]
    m = mod_ref[0]
    logit = []
    for r in rows:
        a = _mm(ya_ref[r, :], wa_ref[...])
        b = _mm(yb_ref[r, :], wb_ref[...])
        merged = _sigmoid(ga_ref[r, :].astype(F32)) * a + _sigmoid(gb_ref[r, :].astype(F32)) * b
        x1 = x_ref[r, :] + m[2:3] * _mm(merged.astype(BF16), wo_ref[...])
        x1_ref[r, :] = x1
        y = x1 * lax.rsqrt(jnp.mean(x1 * x1, axis=-1, keepdims=True) + EPS) * n2_ref[...]
        h2 = y * (1.0 + m[4:5]) + m[3:4]
        h2_ref[r.start // SUBLANES:r.stop // SUBLANES] = h2.reshape(-1, SUBLANES, D_MODEL)
        logit.append(_nt(rwt_ref[...], h2, precision=HIGHEST) + rb_ref[...])

    ie = lax.broadcasted_iota(I32, (N_EXPERTS, ts), 0).astype(F32)
    carry = carry_s[:, 0:1]
    wts, idx_out, rank_out = [], [], []
    for s in subs:
        logits = logit[s]
        idxs, vals = [], []
        for _ in range(TOP_K):
            mx = jnp.max(logits, axis=0, keepdims=True)
            am = jnp.min(jnp.where(logits == mx, ie, float(N_EXPERTS)), axis=0, keepdims=True)
            idxs.append(am)
            vals.append(mx)
            logits = jnp.where(ie == am, -jnp.inf, logits)
        exps = [jnp.exp(v - vals[0]) for v in vals]
        inv = 1.0 / (exps[0] + exps[1] + exps[2] + exps[3])
        wts.append(jnp.concatenate([e * inv for e in exps], axis=0))
        idx_out.append(jnp.concatenate(idxs, axis=0).astype(I32))

        onehot = jnp.zeros((N_EXPERTS, ts), F32)
        for am in idxs:
            onehot = onehot + (ie == am).astype(F32)
        before = _mm(onehot.astype(BF16), su_ref[...]) + carry
        ranks = [jnp.sum(jnp.where(ie == am, before, 0.0), axis=0, keepdims=True) for am in idxs]
        rank_out.append(jnp.concatenate(ranks, axis=0).astype(I32))
        carry = carry + jnp.sum(onehot, axis=1, keepdims=True)
    wt_ref[...] = jnp.concatenate(wts, axis=1)
    idx_ref[...] = jnp.concatenate(idx_out, axis=1)
    rank_ref[...] = jnp.concatenate(rank_out, axis=1)
    carry_s[...] = jnp.broadcast_to(carry, carry_s.shape)
    cnt_ref[...] = carry_s[...]


def _merge(ya, yb, proj, x2, mod3, w_a, w_b, w_o, norm2_g, router_wt, router_b, su, seq):
    n_tok = x2.shape[0]
    tm = min(ROW_TILE_MERGE, seq)
    d = D_MODEL
    gcol = 7 * d // d
    row = lambda j: pl.BlockSpec((tm, d), lambda i: (i, j))
    const = lambda shape: pl.BlockSpec(shape, lambda i: tuple(0 for _ in shape))
    tok4 = pl.BlockSpec((TOP_K, tm), lambda i: (0, i))
    return pl.pallas_call(
        _merge_kernel,
        grid=(n_tok // tm,),
        in_specs=[
            row(0), row(0), row(gcol), row(gcol + 1), row(0),
            pl.BlockSpec((1, N_MOD, d), lambda i: ((i * tm) // seq, 0, 0)),
            const((d, d)), const((d, d)), const((d, d)), const((1, d)),
            const((N_EXPERTS, d)), const((N_EXPERTS, 1)), const((tm // MERGE_SUB, tm // MERGE_SUB)),
        ],
        out_specs=[
            row(0), pl.BlockSpec((tm // SUBLANES, SUBLANES, d), lambda i: (i, 0, 0)),
            tok4, tok4, tok4,
            const((N_EXPERTS, LANES)),
        ],
        out_shape=[
            jax.ShapeDtypeStruct((n_tok, d), F32),
            jax.ShapeDtypeStruct((n_tok // SUBLANES, SUBLANES, d), F32),
            jax.ShapeDtypeStruct((TOP_K, n_tok), I32),
            jax.ShapeDtypeStruct((TOP_K, n_tok), F32),
            jax.ShapeDtypeStruct((TOP_K, n_tok), I32),
            jax.ShapeDtypeStruct((N_EXPERTS, LANES), F32),
        ],
        scratch_shapes=[pltpu.VMEM((N_EXPERTS, LANES), F32)],
        compiler_params=pltpu.CompilerParams(dimension_semantics=("arbitrary",), vmem_limit_bytes=VMEM_LIMIT),
        name="merge",
    )(ya, yb, proj, proj, x2, mod3, w_a, w_b, w_o, norm2_g.reshape(1, d), router_wt, router_b.reshape(-1, 1), su)


def _scatter_kernel(start_ref, end_ref, nb_ref, dest_ref, h2_ref, xs_ref, zero_s, sem, zsem):
    tm = h2_ref.shape[0] * h2_ref.shape[1]
    bm = zero_s.shape[0]

    @pl.when(pl.program_id(0) == 0)
    def _():
        zero_s[...] = jnp.zeros_like(zero_s)

        def zero_block(r0):
            pltpu.make_async_copy(zero_s, xs_ref.at[pl.ds(pl.multiple_of(r0, bm), bm), :], zsem).start()

        def per_expert(e, n):
            lo, hi = start_ref[e], end_ref[e]

            @pl.when(hi > lo)
            def _():
                zero_block(hi - bm)

            return n + (hi > lo).astype(I32)

        n_fill = lax.fori_loop(0, N_EXPERTS, per_expert, 0)
        n_blocks = xs_ref.shape[0] // bm

        def fill_block(j, c):
            zero_block(j * bm)
            return c

        lax.fori_loop(nb_ref[0], n_blocks, fill_block, 0)

        def drain_block(j, c):
            pltpu.make_async_copy(zero_s, xs_ref.at[pl.ds(0, bm), :], zsem).wait()
            return c

        lax.fori_loop(0, n_fill + n_blocks - nb_ref[0], drain_block, 0)

    def send(g, c):
        for u in range(DMA_UNROLL):
            t = g * DMA_UNROLL + u
            for r in range(TOP_K):
                pltpu.make_async_copy(
                    h2_ref.at[g, pl.ds(u, 1), :], xs_ref.at[pl.ds(dest_ref[r, t], 1), :], sem).start()
        return c

    lax.fori_loop(0, tm // DMA_UNROLL, send, 0)

    def drain(g, c):
        for _ in range(DMA_UNROLL * TOP_K):
            pltpu.make_async_copy(h2_ref.at[0, pl.ds(0, 1), :], xs_ref.at[pl.ds(0, 1), :], sem).wait()
        return c

    lax.fori_loop(0, tm // DMA_UNROLL, drain, 0)


def _scatter(start, end, n_used, dest, h2s, n_rows, seq):
    n_tok = h2s.shape[0] * h2s.shape[1]
    tm = min(MOE_TOKEN_TILE, seq)
    smem = lambda: pl.BlockSpec(memory_space=pltpu.SMEM)
    tok4 = pl.BlockSpec((TOP_K, tm), lambda i: (0, i), memory_space=pltpu.SMEM)
    return pl.pallas_call(
        _scatter_kernel,
        grid=(n_tok // tm,),
        in_specs=[smem(), smem(), smem(), tok4,
                  pl.BlockSpec((tm // SUBLANES, SUBLANES, D_MODEL), lambda i: (i, 0, 0))],
        out_specs=pl.BlockSpec(memory_space=pl.ANY),
        out_shape=jax.ShapeDtypeStruct((n_rows, D_MODEL), F32),
        scratch_shapes=[pltpu.VMEM((ROW_TILE_MOE, D_MODEL), F32), pltpu.SemaphoreType.DMA,
                        pltpu.SemaphoreType.DMA],
        compiler_params=pltpu.CompilerParams(dimension_semantics=("arbitrary",)),
        name="scatter",
    )(start, end, n_used, dest, h2s)


def _expert_kernel(be_ref, nb_ref, xs_ref, w1_ref, b1_ref, w2_ref, b2_ref, ys_ref, w1_s, w2_s):
    j = pl.program_id(0)
    changed = jnp.logical_or(j == 0, be_ref[j] != be_ref[jnp.maximum(j - 1, 0)])

    @pl.when(changed)
    def _():
        w1_s[...] = w1_ref[...].astype(BF16)
        w2_s[...] = w2_ref[...].astype(BF16)

    @pl.when(j < nb_ref[0])
    def _():
        a = _mm(xs_ref[...].astype(BF16), w1_s[...]) + b1_ref[...]
        glu = jnp.minimum(a[:, :D_FF], SWIGLU_LIMIT)
        lin = jnp.clip(a[:, D_FF:], -SWIGLU_LIMIT, SWIGLU_LIMIT)
        hmid = (lin + 1.0) * (glu * _sigmoid(SWIGLU_ALPHA * glu))
        ys_ref[...] = _mm(hmid.astype(BF16), w2_s[...]) + b2_ref[...]

    @pl.when(j >= nb_ref[0])
    def _():
        ys_ref[...] = jnp.zeros_like(ys_ref)


def _experts(block_e, n_blocks_used, xs, w1, b1, w2, b2):
    n_rows = xs.shape[0]
    bm = ROW_TILE_MOE
    d = D_MODEL
    blk = lambda j, be, nb: (j, 0)
    exp = lambda j, be, nb: (be[j], 0, 0)
    return pl.pallas_call(
        _expert_kernel,
        grid_spec=pltpu.PrefetchScalarGridSpec(
            num_scalar_prefetch=2,
            grid=(n_rows // bm,),
            in_specs=[
                pl.BlockSpec((bm, d), blk),
                pl.BlockSpec((None, d, 2 * D_FF), exp),
                pl.BlockSpec((None, 1, 2 * D_FF), exp),
                pl.BlockSpec((None, D_FF, d), exp),
                pl.BlockSpec((None, 1, d), exp),
            ],
            out_specs=pl.BlockSpec((bm, d), blk),
            scratch_shapes=[pltpu.VMEM((d, 2 * D_FF), BF16), pltpu.VMEM((D_FF, d), BF16)],
        ),
        out_shape=jax.ShapeDtypeStruct((n_rows, d), F32),
        compiler_params=pltpu.CompilerParams(dimension_semantics=("arbitrary",), vmem_limit_bytes=VMEM_LIMIT),
        name="experts",
    )(block_e, n_blocks_used, xs, w1, b1.reshape(N_EXPERTS, 1, -1), w2, b2.reshape(N_EXPERTS, 1, -1))


def _combine_kernel(dest_ref, wt_ref, ys_ref, x1_ref, mod_ref, fg_ref, o_ref, buf_s, sem):
    tm = x1_ref.shape[0]

    def fetch(g, c):
        for u in range(DMA_UNROLL):
            t = g * DMA_UNROLL + u
            for r in range(TOP_K):
                pltpu.make_async_copy(
                    ys_ref.at[pl.ds(dest_ref[r, t], 1), :], buf_s.at[r, g, pl.ds(u, 1), :], sem).start()
        return c

    lax.fori_loop(0, tm // DMA_UNROLL, fetch, 0)

    def drain(g, c):
        for _ in range(DMA_UNROLL * TOP_K):
            pltpu.make_async_copy(ys_ref.at[pl.ds(0, 1), :], buf_s.at[0, 0, pl.ds(0, 1), :], sem).wait()
        return c

    lax.fori_loop(0, tm // DMA_UNROLL, drain, 0)

    wt = wt_ref[...]
    rows = lambda r: buf_s[r].reshape(tm, D_MODEL)
    moe = wt[:, 0:1] * rows(0)
    for r in range(1, TOP_K):
        moe = moe + wt[:, r:r + 1] * rows(r)
    x2 = x1_ref[...] + mod_ref[0][5:6] * moe
    o_ref[...] = x2 * lax.rsqrt(jnp.mean(x2 * x2, axis=-1, keepdims=True) + EPS) * fg_ref[...]


def _combine(dest, wt, ys, x1, mod3, final_g, seq):
    assert DMA_UNROLL == SUBLANES
    n_tok = x1.shape[0]
    tm = min(MOE_TOKEN_TILE, seq)
    d = D_MODEL
    tok4 = pl.BlockSpec((TOP_K, tm), lambda i: (0, i), memory_space=pltpu.SMEM)
    return pl.pallas_call(
        _combine_kernel,
        grid=(n_tok // tm,),
        in_specs=[
            tok4,
            pl.BlockSpec((tm, TOP_K), lambda i: (i, 0)),
            pl.BlockSpec(memory_space=pl.ANY),
            pl.BlockSpec((tm, d), lambda i: (i, 0)),
            pl.BlockSpec((1, N_MOD, d), lambda i: ((i * tm) // seq, 0, 0)),
            pl.BlockSpec((1, d), lambda i: (0, 0)),
        ],
        out_specs=pl.BlockSpec((tm, d), lambda i: (i, 0)),
        out_shape=jax.ShapeDtypeStruct((n_tok, d), F32),
        scratch_shapes=[pltpu.VMEM((TOP_K, tm // DMA_UNROLL, DMA_UNROLL, d), F32), pltpu.SemaphoreType.DMA],
        compiler_params=pltpu.CompilerParams(dimension_semantics=("arbitrary",), vmem_limit_bytes=VMEM_LIMIT),
        name="combine",
    )(dest, wt.T, ys, x1, mod3, final_g.reshape(1, d))


def _layer(x2, mod3, bsz, seq, norm1_g, w_in, conv_w, conv_b, ml_b_i, ml_b_f, ml_norm_g, w_branch_a, w_branch_b,
           w_out, norm2_g, router_w, router_b, expert_w1, expert_b1, expert_w2, expert_b2):
    d = D_MODEL
    n_tok = bsz * seq
    gate0 = 7 * d
    n_gate = 2 * ML_HEADS
    w_main = jnp.concatenate([w_in[:, :gate0], w_in[:, gate0 + n_gate:]], axis=1).astype(BF16)
    w_gate = jnp.pad(w_in[:, gate0:gate0 + n_gate], ((0, 0), (0, LANES - n_gate))).astype(BF16)
    gate_bias = jnp.pad(jnp.concatenate([ml_b_i, ml_b_f]), (0, LANES - n_gate)).reshape(1, LANES)

    proj, gates = _proj(x2, mod3, norm1_g, w_main, w_gate, seq)

    ones = jnp.ones((LANES, LANES), F32)
    uu = jnp.concatenate([jnp.tril(ones), ones], axis=1).astype(BF16)
    ya = _sb_attention(proj, uu, bsz, seq)

    cl = min(ML_CHUNK, seq)
    tri = jnp.tril(jnp.ones((cl, cl), F32))
    yb = _mlstm(proj, gates, gate_bias, conv_w, conv_b, ml_norm_g, tri, bsz, seq)

    ts = min(ROW_TILE_MERGE, seq) // MERGE_SUB
    su = jnp.triu(jnp.ones((ts, ts), F32), k=1).astype(BF16)
    x1, h2s, idx, wt, rank, cnt = _merge(
        ya, yb, proj, x2, mod3, w_branch_a.astype(BF16), w_branch_b.astype(BF16), w_out.astype(BF16),
        norm2_g, router_w.T, router_b, su, seq)

    bm = ROW_TILE_MOE
    counts = cnt[:, 0].astype(I32)
    padded = (counts + bm - 1) // bm * bm
    pad_end = jnp.cumsum(padded)
    start = pad_end - padded
    n_rows = (n_tok * TOP_K + N_EXPERTS * (bm - 1)) // bm * bm
    n_blocks = n_rows // bm
    n_used = jnp.maximum(pad_end[-1] // bm, 1).astype(I32)
    blk = jnp.minimum(jnp.arange(n_blocks, dtype=I32), n_used - 1)
    block_e = jnp.minimum(jnp.sum((pad_end[None, :] <= (blk * bm)[:, None]).astype(I32), axis=1), N_EXPERTS - 1)

    onehot = idx[:, :, None] == jnp.arange(N_EXPERTS, dtype=I32)
    dest = rank + jnp.sum(jnp.where(onehot, start, 0), axis=-1)

    xs = _scatter(start, pad_end, n_used.reshape(1), dest, h2s, n_rows, seq)
    ys = _experts(block_e, n_used.reshape(1), xs, expert_w1, expert_b1, expert_w2, expert_b2)
    return dest, wt, ys, x1


def kernel(x, c, ada_w, ada_b, norm1_g, w_in, conv_w, conv_b, ml_b_i, ml_b_f, ml_norm_g, w_branch_a, w_branch_b, w_out, norm2_g, router_w, router_b, expert_w1, expert_b1, expert_w2, expert_b2, final_g):
    bsz, seq, d = x.shape
    depth = ada_w.shape[0]
    assert d == D_MODEL and depth == 1 and seq % QUERY_TILE == 0
    x2 = x.reshape(bsz * seq, d)
    mod3 = _ada(c, ada_w[0], ada_b[0]).reshape(bsz, N_MOD, d)
    dest, wt, ys, x1 = _layer(
        x2, mod3, bsz, seq, norm1_g[0], w_in[0], conv_w[0], conv_b[0], ml_b_i[0], ml_b_f[0], ml_norm_g[0],
        w_branch_a[0], w_branch_b[0], w_out[0], norm2_g[0], router_w[0], router_b[0],
        expert_w1[0], expert_b1[0], expert_w2[0], expert_b2[0])
    out = _combine(dest, wt, ys, x1, mod3, final_g, seq)
    return out.reshape(bsz, seq, d)
```

```python
import functools

import jax
import jax.numpy as jnp
from jax import lax
from jax.experimental import pallas as pl
from jax.experimental.pallas import tpu as pltpu

F32 = jnp.float32
BF16 = jnp.bfloat16
I32 = jnp.int32
HIGHEST = lax.Precision.HIGHEST

D_MODEL = 1024
SB_HEAD_DIM = 64
ML_HEADS = 4
ML_HEAD_DIM = 256
CONV_WIDTH = 4
N_EXPERTS = 32
TOP_K = 4
D_FF = 1024
SWIGLU_LIMIT = 7.0
SWIGLU_ALPHA = 1.702
N_MOD = 6
EPS = 1e-6
LOG2E = 1.4426950408889634

LANES = 128
SUBLANES = 8
DMA_UNROLL = 8

QUERY_TILE = 128
SB_Q_SPAN = 512
SB_K_SPAN = 512
ML_CHUNK = 256
ML_GROUP = 4
ROW_TILE_PROJ = 1024
ROW_TILE_MERGE = 512
MERGE_SUB = 1
ROW_TILE_MOE = 512
MOE_TOKEN_TILE = 512
VMEM_LIMIT = 56 * 1024 * 1024


def _nt(a, b, precision=None):
    return lax.dot_general(a, b, (((1,), (1,)), ((), ())), preferred_element_type=F32, precision=precision)


def _mm(a, b, precision=None):
    return jnp.dot(a, b, preferred_element_type=F32, precision=precision)


def _sigmoid(x):
    return 1.0 / (1.0 + jnp.exp(-x))


def _log_sigmoid(x):
    return jnp.minimum(x, 0.0) - jnp.log(1.0 + jnp.exp(-jnp.abs(x)))


def _ada_kernel(c_ref, w_ref, b_ref, o_ref):
    c = c_ref[...]
    o_ref[...] = _mm(c * _sigmoid(c), w_ref[...], precision=HIGHEST) + b_ref[...]


def _ada(c, ada_w, ada_b):
    bsz = c.shape[0]
    n = ada_w.shape[1]
    return pl.pallas_call(
        _ada_kernel,
        grid=(n // D_MODEL,),
        in_specs=[
            pl.BlockSpec((bsz, D_MODEL), lambda j: (0, 0)),
            pl.BlockSpec((D_MODEL, D_MODEL), lambda j: (0, j)),
            pl.BlockSpec((1, D_MODEL), lambda j: (0, j)),
        ],
        out_specs=pl.BlockSpec((bsz, D_MODEL), lambda j: (0, j)),
        out_shape=jax.ShapeDtypeStruct((bsz, n), F32),
        name="ada",
    )(c, ada_w, ada_b.reshape(1, n))


def _proj_kernel(x_ref, mod_ref, g_ref, w_ref, wg_ref, o_ref, og_ref, h_ref):
    @pl.when(pl.program_id(1) == 0)
    def _():
        x = x_ref[...]
        y = x * lax.rsqrt(jnp.mean(x * x, axis=-1, keepdims=True) + EPS) * g_ref[...]
        m = mod_ref[0]
        hb = (y * (1.0 + m[1:2]) + m[0:1]).astype(BF16)
        h_ref[...] = hb
        og_ref[...] = _mm(hb, wg_ref[...])

    o_ref[...] = _mm(h_ref[...], w_ref[...]).astype(BF16)


def _proj(x2, mod3, norm_g, w_main, w_gate, seq):
    n_tok = x2.shape[0]
    n_main = w_main.shape[1]
    tm = min(ROW_TILE_PROJ, seq)
    tn = D_MODEL
    return pl.pallas_call(
        _proj_kernel,
        grid=(n_tok // tm, n_main // tn),
        in_specs=[
            pl.BlockSpec((tm, D_MODEL), lambda i, j: (i, 0)),
            pl.BlockSpec((1, N_MOD, D_MODEL), lambda i, j: ((i * tm) // seq, 0, 0)),
            pl.BlockSpec((1, D_MODEL), lambda i, j: (0, 0)),
            pl.BlockSpec((D_MODEL, tn), lambda i, j: (0, j)),
            pl.BlockSpec((D_MODEL, LANES), lambda i, j: (0, 0)),
        ],
        out_specs=[
            pl.BlockSpec((tm, tn), lambda i, j: (i, j)),
            pl.BlockSpec((tm, LANES), lambda i, j: (i, 0)),
        ],
        out_shape=[
            jax.ShapeDtypeStruct((n_tok, n_main), BF16),
            jax.ShapeDtypeStruct((n_tok, LANES), F32),
        ],
        scratch_shapes=[pltpu.VMEM((tm, D_MODEL), BF16)],
        compiler_params=pltpu.CompilerParams(
            dimension_semantics=("arbitrary", "arbitrary"), vmem_limit_bytes=VMEM_LIMIT),
        name="proj",
    )(x2, mod3, norm_g.reshape(1, D_MODEL), w_main, w_gate)


def _sb_kernel(q_ref, k_ref, v_ref, uu_ref, o_ref, qh_s, run_s, acc_s):
    seq = q_ref.shape[0]
    qt = QUERY_TILE
    tq, tk = min(SB_Q_SPAN, seq), min(SB_K_SPAN, seq)
    n_heads = LANES // SB_HEAD_DIM
    n_qb, n_kb = tq // qt, tk // qt
    lane = lax.broadcasted_iota(I32, (qt, LANES), 1)
    strict = lane < lax.broadcasted_iota(I32, (qt, LANES), 0)
    uu = uu_ref[...]

    def group(s0, diag):
        k = k_ref[pl.ds(s0, tk), :]
        v = v_ref[pl.ds(s0, tk), :]
        chains = range(n_qb * n_heads)
        nbs = [ci // n_heads + 1 if diag else n_kb for ci in chains]
        blk = lambda x, b: x[:, b * qt:(b + 1) * qt]
        zs = [_nt(qh_s[ci], k[:nbs[ci] * qt]) for ci in chains]
        lkb = []
        for ci in chains:
            z = zs[ci]
            nz = -z
            lk = jnp.minimum(nz, 0.0) - jnp.log(1.0 + jnp.exp2(jnp.minimum(z, nz))) * LOG2E
            lks = [blk(lk, b) for b in range(nbs[ci])]
            if diag:
                lks[-1] = jnp.where(strict, lks[-1], 0.0)
            lkb.append([x.astype(BF16) for x in lks])
        ccs = [[_mm(lkb[ci][b], uu) for b in range(nbs[ci])] for ci in chains]
        probs = []
        for ci in chains:
            run = run_s[ci]
            a = [None] * nbs[ci]
            for b in reversed(range(nbs[ci])):
                a[b] = jnp.exp2(blk(zs[ci], b) + ccs[ci][b][:, :LANES] + run)
                run = run + ccs[ci][b][:, LANES:]
            run_s[ci] = run
            if diag:
                a[-1] = jnp.where(strict, a[-1], 0.0)
            probs.append(jnp.concatenate(a, axis=1).astype(BF16))
        for ci in chains:
            acc_s[ci] = acc_s[ci] + _mm(probs[ci], v[:nbs[ci] * qt])

    def qspan(i, carry):
        r0 = pl.multiple_of(i * tq, tq)
        run_s[...] = jnp.zeros_like(run_s)
        acc_s[...] = jnp.zeros_like(acc_s)
        for qb in range(n_qb):
            qf = q_ref[pl.ds(r0 + qb * qt, qt), :].astype(F32) * (SB_HEAD_DIM ** -0.5 * LOG2E)
            for h in range(n_heads):
                in_head = (lane >= h * SB_HEAD_DIM) & (lane < (h + 1) * SB_HEAD_DIM)
                qh_s[qb * n_heads + h] = jnp.where(in_head, qf, 0.0).astype(BF16)

        group(r0, True)

        def past(j, c):
            group(pl.multiple_of((i - 1 - j) * tk, tk), False)
            return c

        lax.fori_loop(0, i, past, 0)
        for qb in range(n_qb):
            o_ref[pl.ds(r0 + qb * qt, qt), :] = jnp.where(
                lane < SB_HEAD_DIM, acc_s[qb * n_heads], acc_s[qb * n_heads + 1]).astype(BF16)
        return carry

    lax.fori_loop(0, seq // tq, qspan, 0)


def _sb_attention(proj, uu, bsz, seq):
    n_pairs = D_MODEL // LANES
    tq, tk = min(SB_Q_SPAN, seq), min(SB_K_SPAN, seq)
    assert tk == tq and seq % tk == 0 and LANES // SB_HEAD_DIM == 2
    n_chain = (tq // QUERY_TILE) * (LANES // SB_HEAD_DIM)
    return pl.pallas_call(
        _sb_kernel,
        grid=(bsz, n_pairs),
        in_specs=[
            pl.BlockSpec((seq, LANES), lambda b, p: (b, p)),
            pl.BlockSpec((seq, LANES), lambda b, p: (b, n_pairs + p)),
            pl.BlockSpec((seq, LANES), lambda b, p: (b, 2 * n_pairs + p)),
            pl.BlockSpec((LANES, 2 * LANES), lambda b, p: (0, 0)),
        ],
        out_specs=pl.BlockSpec((seq, LANES), lambda b, p: (b, p)),
        out_shape=jax.ShapeDtypeStruct((bsz * seq, D_MODEL), BF16),
        scratch_shapes=[
            pltpu.VMEM((n_chain, QUERY_TILE, LANES), BF16),
            pltpu.VMEM((n_chain, QUERY_TILE, LANES), F32),
            pltpu.VMEM((n_chain, QUERY_TILE, LANES), F32),
        ],
        compiler_params=pltpu.CompilerParams(dimension_semantics=("arbitrary", "arbitrary")),
        name="sb_attn",
    )(proj, proj, proj, uu)


def _mlstm_kernel(q_ref, k_ref, v_ref, og_ref, gt_ref, gb_ref, cwq_ref, cwk_ref, cbq_ref, cbk_ref, ng_ref, tri_ref,
                  y_ref, xq_s, xk_s, c_s, n_s, m_s):
    hd = ML_HEAD_DIM
    heads = range(ML_GROUP)
    first = pl.program_id(1) * ML_GROUP
    seq = q_ref.shape[0]
    cl = min(ML_CHUNK, seq)
    pad = SUBLANES
    for h in heads:
        xq_s[h, 0:pad, :] = jnp.zeros((pad, hd), F32)
        xk_s[h, 0:pad, :] = jnp.zeros((pad, hd), F32)
    c_s[...] = jnp.zeros_like(c_s)
    n_s[...] = jnp.zeros_like(n_s)
    m_s[...] = jnp.zeros_like(m_s)
    lane = lax.broadcasted_iota(I32, (cl, LANES), 1)
    sel_i = [(lane == first + h).astype(F32) for h in heads]
    sel_f = [(lane == first + h + ML_HEADS).astype(F32) for h in heads]
    tril = lax.broadcasted_iota(I32, (cl, cl), 1) <= lax.broadcasted_iota(I32, (cl, cl), 0)
    cols = lambda h: slice(h * hd, (h + 1) * hd)

    def conv_silu(x_s, h, cw_ref, cb_ref):
        y = cb_ref[:, cols(h)]
        for tap in range(CONV_WIDTH):
            lo = pad - (CONV_WIDTH - 1) + tap
            y = y + x_s[h, lo:lo + cl, :] * cw_ref[tap:tap + 1, cols(h)]
        x_s[h, 0:pad, :] = x_s[h, cl:cl + pad, :]
        return y * _sigmoid(y)

    def chunk(c, carry):
        rows = pl.ds(pl.multiple_of(c * cl, cl), cl)
        for h in heads:
            xq_s[h, pad:pad + cl, :] = q_ref[rows, cols(h)].astype(F32)
            xk_s[h, pad:pad + cl, :] = k_ref[rows, cols(h)].astype(F32)
        qc = [conv_silu(xq_s, h, cwq_ref, cbq_ref) for h in heads]
        kc = [conv_silu(xk_s, h, cwk_ref, cbk_ref) * (hd ** -0.5) for h in heads]
        qb = [x.astype(BF16) for x in qc]
        kb = [x.astype(BF16) for x in kc]
        vb = [v_ref[rows, cols(h)] for h in heads]

        pre = gt_ref[rows, :] + gb_ref[...]
        bt = _mm(tri_ref[...], _log_sigmoid(pre), precision=HIGHEST)
        b_col = [jnp.sum(bt * sel_f[h], axis=1, keepdims=True) for h in heads]
        i_col = [jnp.sum(pre * sel_i[h], axis=1, keepdims=True) for h in heads]
        g_col = [i_col[h] - b_col[h] for h in heads]
        g_row = [jnp.transpose(jnp.broadcast_to(g_col[h], (cl, LANES)))[0:1, :] for h in heads]

        m_prev = [m_s[h] for h in heads]
        dmat = [jnp.where(tril, b_col[h] + g_row[h], -jnp.inf) for h in heads]
        m_row = [jnp.maximum(b_col[h] + m_prev[h], jnp.max(dmat[h], axis=1, keepdims=True)) for h in heads]
        w_inter = [jnp.exp(b_col[h] + m_prev[h] - m_row[h]) for h in heads]
        qk = [_nt(qb[h], kb[h]) for h in heads]
        s_mat = [qk[h] * jnp.exp(dmat[h] - m_row[h]) for h in heads]
        q_c = [_mm(qb[h], c_s[h].astype(BF16)) for h in heads]
        s_v = [_mm(s_mat[h].astype(BF16), vb[h]) for h in heads]
        den = [w_inter[h] * jnp.sum(qc[h] * n_s[h], axis=1, keepdims=True) + jnp.sum(s_mat[h], axis=1, keepdims=True)
               for h in heads]
        hh = [(w_inter[h] * q_c[h] + s_v[h]) * (1.0 / jnp.maximum(jnp.abs(den[h]), jnp.exp(-m_row[h])))
              for h in heads]

        m_new = [m_row[h][cl - 1:cl, :] for h in heads]
        b_last = [b_col[h][cl - 1:cl, :] for h in heads]
        decay = [jnp.exp(b_last[h] + m_prev[h] - m_new[h]) for h in heads]
        wk = [jnp.exp(b_last[h] + g_col[h] - m_new[h]) * kc[h] for h in heads]
        k_v = [lax.dot_general(wk[h].astype(BF16), vb[h], (((0,), (0,)), ((), ())), preferred_element_type=F32)
               for h in heads]
        for h in heads:
            c_s[h] = decay[h] * c_s[h] + k_v[h]
            n_s[h] = decay[h] * n_s[h] + jnp.sum(wk[h], axis=0, keepdims=True)
            m_s[h] = m_new[h]
        for h in heads:
            hn = hh[h] * lax.rsqrt(jnp.mean(hh[h] * hh[h], axis=1, keepdims=True) + EPS) * ng_ref[:, cols(h)]
            y_ref[rows, cols(h)] = (hn * _sigmoid(og_ref[rows, cols(h)].astype(F32))).astype(BF16)
        return carry

    lax.fori_loop(0, seq // cl, chunk, 0)


def _mlstm(proj, gates, gate_bias, conv_w, conv_b, norm_g, tri, bsz, seq):
    hd = ML_HEAD_DIM
    g = ML_GROUP
    cl = min(ML_CHUNK, seq)
    n_grp = ML_HEADS // g
    col0 = 3 * D_MODEL // (g * hd)
    row_spec = lambda off: pl.BlockSpec((seq, g * hd), lambda b, j: (b, off + j))
    return pl.pallas_call(
        _mlstm_kernel,
        grid=(bsz, n_grp),
        in_specs=[
            row_spec(col0), row_spec(col0 + n_grp), row_spec(col0 + 2 * n_grp), row_spec(col0 + 3 * n_grp),
            pl.BlockSpec((seq, LANES), lambda b, j: (b, 0)),
            pl.BlockSpec((1, LANES), lambda b, j: (0, 0)),
            pl.BlockSpec((CONV_WIDTH, g * hd), lambda b, j: (0, j)),
            pl.BlockSpec((CONV_WIDTH, g * hd), lambda b, j: (0, n_grp + j)),
            pl.BlockSpec((1, g * hd), lambda b, j: (0, j)),
            pl.BlockSpec((1, g * hd), lambda b, j: (0, n_grp + j)),
            pl.BlockSpec((1, g * hd), lambda b, j: (0, j)),
            pl.BlockSpec((cl, cl), lambda b, j: (0, 0)),
        ],
        out_specs=pl.BlockSpec((seq, g * hd), lambda b, j: (b, j)),
        out_shape=jax.ShapeDtypeStruct((bsz * seq, D_MODEL), BF16),
        scratch_shapes=[
            pltpu.VMEM((g, cl + 2 * SUBLANES, hd), F32),
            pltpu.VMEM((g, cl + 2 * SUBLANES, hd), F32),
            pltpu.VMEM((g, hd, hd), F32),
            pltpu.VMEM((g, 1, hd), F32),
            pltpu.VMEM((g, 1, 1), F32),
        ],
        compiler_params=pltpu.CompilerParams(
            dimension_semantics=("arbitrary", "arbitrary"), vmem_limit_bytes=VMEM_LIMIT),
        name="mlstm",
    )(proj, proj, proj, proj, gates, gate_bias, conv_w, conv_w,
      conv_b.reshape(1, -1), conv_b.reshape(1, -1), norm_g.reshape(1, -1), tri)


def _merge_kernel(ya_ref, yb_ref, ga_ref, gb_ref, x_ref, mod_ref, wa_ref, wb_ref, wo_ref, n2_ref, rwt_ref, rb_ref,
                  su_ref, x1_ref, h2_ref, idx_ref, wt_ref, rank_ref, cnt_ref, carry_s):
    @pl.when(pl.program_id(0) == 0)
    def _():
        carry_s[...] = jnp.zeros_like(carry_s)

    ts = x_ref.shape[0] // MERGE_SUB
    subs = range(MERGE_SUB)
    rows = [slice(s * ts, (s + 1) * ts) for s in subs]
    m = mod_ref[0]
    logit = []
    for r in rows:
        a = _mm(ya_ref[r, :], wa_ref[...])
        b = _mm(yb_ref[r, :], wb_ref[...])
        merged = _sigmoid(ga_ref[r, :].astype(F32)) * a + _sigmoid(gb_ref[r, :].astype(F32)) * b
        x1 = x_ref[r, :] + m[2:3] * _mm(merged.astype(BF16), wo_ref[...])
        x1_ref[r, :] = x1
        y = x1 * lax.rsqrt(jnp.mean(x1 * x1, axis=-1, keepdims=True) + EPS) * n2_ref[...]
        h2 = y * (1.0 + m[4:5]) + m[3:4]
        h2_ref[r, :] = h2
        logit.append(_nt(rwt_ref[...], h2, precision=HIGHEST) + rb_ref[...])

    ie = lax.broadcasted_iota(I32, (N_EXPERTS, ts), 0).astype(F32)
    carry = carry_s[:, 0:1]
    wts, idx_out, rank_out = [], [], []
    for s in subs:
        logits = logit[s]
        idxs, vals = [], []
        for _ in range(TOP_K):
            mx = jnp.max(logits, axis=0, keepdims=True)
            am = jnp.min(jnp.where(logits == mx, ie, float(N_EXPERTS)), axis=0, keepdims=True)
            idxs.append(am)
            vals.append(mx)
            logits = jnp.where(ie == am, -jnp.inf, logits)
        exps = [jnp.exp(v - vals[0]) for v in vals]
        inv = 1.0 / (exps[0] + exps[1] + exps[2] + exps[3])
        wts.append(jnp.concatenate([e * inv for e in exps], axis=0))
        idx_out.append(jnp.concatenate(idxs, axis=0).astype(I32))

        onehot = jnp.zeros((N_EXPERTS, ts), F32)
        for am in idxs:
            onehot = onehot + (ie == am).astype(F32)
        before = _mm(onehot.astype(BF16), su_ref[...]) + carry
        ranks = [jnp.sum(jnp.where(ie == am, before, 0.0), axis=0, keepdims=True) for am in idxs]
        rank_out.append(jnp.concatenate(ranks, axis=0).astype(I32))
        carry = carry + jnp.sum(onehot, axis=1, keepdims=True)
    wt_ref[...] = jnp.concatenate(wts, axis=1)
    idx_ref[...] = jnp.concatenate(idx_out, axis=1)
    rank_ref[...] = jnp.concatenate(rank_out, axis=1)
    carry_s[...] = jnp.broadcast_to(carry, carry_s.shape)
    cnt_ref[...] = carry_s[...]


def _merge(ya, yb, proj, x2, mod3, w_a, w_b, w_o, norm2_g, router_wt, router_b, su, seq):
    n_tok = x2.shape[0]
    tm = min(ROW_TILE_MERGE, seq)
    d = D_MODEL
    gcol = 7 * d // d
    row = lambda j: pl.BlockSpec((tm, d), lambda i: (i, j))
    const = lambda shape: pl.BlockSpec(shape, lambda i: tuple(0 for _ in shape))
    tok4 = pl.BlockSpec((TOP_K, tm), lambda i: (0, i))
    return pl.pallas_call(
        _merge_kernel,
        grid=(n_tok // tm,),
        in_specs=[
            row(0), row(0), row(gcol), row(gcol + 1), row(0),
            pl.BlockSpec((1, N_MOD, d), lambda i: ((i * tm) // seq, 0, 0)),
            const((d, d)), const((d, d)), const((d, d)), const((1, d)),
            const((N_EXPERTS, d)), const((N_EXPERTS, 1)), const((tm // MERGE_SUB, tm // MERGE_SUB)),
        ],
        out_specs=[
            row(0), row(0),
            tok4, tok4, tok4,
            const((N_EXPERTS, LANES)),
        ],
        out_shape=[
            jax.ShapeDtypeStruct((n_tok, d), F32),
            jax.ShapeDtypeStruct((n_tok, d), F32),
            jax.ShapeDtypeStruct((TOP_K, n_tok), I32),
            jax.ShapeDtypeStruct((TOP_K, n_tok), F32),
            jax.ShapeDtypeStruct((TOP_K, n_tok), I32),
            jax.ShapeDtypeStruct((N_EXPERTS, LANES), F32),
        ],
        scratch_shapes=[pltpu.VMEM((N_EXPERTS, LANES), F32)],
        compiler_params=pltpu.CompilerParams(dimension_semantics=("arbitrary",), vmem_limit_bytes=VMEM_LIMIT),
        name="merge",
    )(ya, yb, proj, proj, x2, mod3, w_a, w_b, w_o, norm2_g.reshape(1, d), router_wt, router_b.reshape(-1, 1), su)


def _scatter_kernel(start_ref, end_ref, nb_ref, dest_ref, h2_ref, xs_ref, zero_s, sem, zsem):
    tm = h2_ref.shape[0]
    bm = zero_s.shape[0]

    @pl.when(pl.program_id(0) == 0)
    def _():
        zero_s[...] = jnp.zeros_like(zero_s)

        def zero_block(r0):
            pltpu.make_async_copy(zero_s, xs_ref.at[pl.ds(pl.multiple_of(r0, bm), bm), :], zsem).start()

        def per_expert(e, n):
            lo, hi = start_ref[e], end_ref[e]

            @pl.when(hi > lo)
            def _():
                zero_block(hi - bm)

            return n + (hi > lo).astype(I32)

        n_fill = lax.fori_loop(0, N_EXPERTS, per_expert, 0)
        n_blocks = xs_ref.shape[0] // bm

        def fill_block(j, c):
            zero_block(j * bm)
            return c

        lax.fori_loop(nb_ref[0], n_blocks, fill_block, 0)

        def drain_block(j, c):
            pltpu.make_async_copy(zero_s, xs_ref.at[pl.ds(0, bm), :], zsem).wait()
            return c

        lax.fori_loop(0, n_fill + n_blocks - nb_ref[0], drain_block, 0)

    def send(g, c):
        for u in range(DMA_UNROLL):
            t = g * DMA_UNROLL + u
            for r in range(TOP_K):
                pltpu.make_async_copy(
                    h2_ref.at[pl.ds(t, 1), :], xs_ref.at[pl.ds(dest_ref[r, t], 1), :], sem).start(priority=r % 2)
        return c

    lax.fori_loop(0, tm // DMA_UNROLL, send, 0)

    def drain(g, c):
        for _ in range(DMA_UNROLL * TOP_K):
            pltpu.make_async_copy(h2_ref.at[pl.ds(0, 1), :], xs_ref.at[pl.ds(0, 1), :], sem).wait()
        return c

    lax.fori_loop(0, tm // DMA_UNROLL, drain, 0)


def _scatter(start, end, n_used, dest, h2s, n_rows, seq):
    n_tok = h2s.shape[0]
    tm = min(MOE_TOKEN_TILE, seq)
    smem = lambda: pl.BlockSpec(memory_space=pltpu.SMEM)
    tok4 = pl.BlockSpec((TOP_K, tm), lambda i: (0, i), memory_space=pltpu.SMEM)
    return pl.pallas_call(
        _scatter_kernel,
        grid=(n_tok // tm,),
        in_specs=[smem(), smem(), smem(), tok4, pl.BlockSpec((tm, D_MODEL), lambda i: (i, 0))],
        out_specs=pl.BlockSpec(memory_space=pl.ANY),
        out_shape=jax.ShapeDtypeStruct((n_rows, D_MODEL), F32),
        scratch_shapes=[pltpu.VMEM((ROW_TILE_MOE, D_MODEL), F32), pltpu.SemaphoreType.DMA,
                        pltpu.SemaphoreType.DMA],
        compiler_params=pltpu.CompilerParams(dimension_semantics=("arbitrary",)),
        name="scatter",
    )(start, end, n_used, dest, h2s)


def _expert_kernel(be_ref, nb_ref, xs_ref, w1_ref, b1_ref, w2_ref, b2_ref, ys_ref, w1_s, w2_s):
    j = pl.program_id(0)
    changed = jnp.logical_or(j == 0, be_ref[j] != be_ref[jnp.maximum(j - 1, 0)])

    @pl.when(changed)
    def _():
        w1_s[...] = w1_ref[...].astype(BF16)
        w2_s[...] = w2_ref[...].astype(BF16)

    @pl.when(j < nb_ref[0])
    def _():
        a = _mm(xs_ref[...].astype(BF16), w1_s[...]) + b1_ref[...]
        glu = jnp.minimum(a[:, :D_FF], SWIGLU_LIMIT)
        lin = jnp.clip(a[:, D_FF:], -SWIGLU_LIMIT, SWIGLU_LIMIT)
        hmid = (lin + 1.0) * (glu * _sigmoid(SWIGLU_ALPHA * glu))
        ys_ref[...] = _mm(hmid.astype(BF16), w2_s[...]) + b2_ref[...]

    @pl.when(j >= nb_ref[0])
    def _():
        ys_ref[...] = jnp.zeros_like(ys_ref)


def _experts(block_e, n_blocks_used, xs, w1, b1, w2, b2):
    n_rows = xs.shape[0]
    bm = ROW_TILE_MOE
    d = D_MODEL
    blk = lambda j, be, nb: (j, 0)
    exp = lambda j, be, nb: (be[j], 0, 0)
    return pl.pallas_call(
        _expert_kernel,
        grid_spec=pltpu.PrefetchScalarGridSpec(
            num_scalar_prefetch=2,
            grid=(n_rows // bm,),
            in_specs=[
                pl.BlockSpec((bm, d), blk),
                pl.BlockSpec((None, d, 2 * D_FF), exp),
                pl.BlockSpec((None, 1, 2 * D_FF), exp),
                pl.BlockSpec((None, D_FF, d), exp),
                pl.BlockSpec((None, 1, d), exp),
            ],
            out_specs=pl.BlockSpec((bm, d), blk),
            scratch_shapes=[pltpu.VMEM((d, 2 * D_FF), BF16), pltpu.VMEM((D_FF, d), BF16)],
        ),
        out_shape=jax.ShapeDtypeStruct((n_rows, d), F32),
        compiler_params=pltpu.CompilerParams(dimension_semantics=("arbitrary",), vmem_limit_bytes=VMEM_LIMIT),
        name="experts",
    )(block_e, n_blocks_used, xs, w1, b1.reshape(N_EXPERTS, 1, -1), w2, b2.reshape(N_EXPERTS, 1, -1))


def _combine_kernel(dest_ref, wt_ref, ys_ref, x1_ref, mod_ref, fg_ref, o_ref, buf_s, sem):
    tm = x1_ref.shape[0]

    def fetch(g, c):
        for u in range(DMA_UNROLL):
            t = g * DMA_UNROLL + u
            for r in range(TOP_K):
                pltpu.make_async_copy(
                    ys_ref.at[pl.ds(dest_ref[r, t], 1), :], buf_s.at[r, g, pl.ds(u, 1), :], sem).start(priority=r % 2)
        return c

    lax.fori_loop(0, tm // DMA_UNROLL, fetch, 0)

    def drain(g, c):
        for _ in range(DMA_UNROLL * TOP_K):
            pltpu.make_async_copy(ys_ref.at[pl.ds(0, 1), :], buf_s.at[0, 0, pl.ds(0, 1), :], sem).wait()
        return c

    lax.fori_loop(0, tm // DMA_UNROLL, drain, 0)

    wt = wt_ref[...]
    rows = lambda r: buf_s[r].reshape(tm, D_MODEL)
    moe = wt[:, 0:1] * rows(0)
    for r in range(1, TOP_K):
        moe = moe + wt[:, r:r + 1] * rows(r)
    x2 = x1_ref[...] + mod_ref[0][5:6] * moe
    o_ref[...] = x2 * lax.rsqrt(jnp.mean(x2 * x2, axis=-1, keepdims=True) + EPS) * fg_ref[...]


def _combine(dest, wt, ys, x1, mod3, final_g, seq):
    assert DMA_UNROLL == SUBLANES
    n_tok = x1.shape[0]
    tm = min(MOE_TOKEN_TILE, seq)
    d = D_MODEL
    tok4 = pl.BlockSpec((TOP_K, tm), lambda i: (0, i), memory_space=pltpu.SMEM)
    return pl.pallas_call(
        _combine_kernel,
        grid=(n_tok // tm,),
        in_specs=[
            tok4,
            pl.BlockSpec((tm, TOP_K), lambda i: (i, 0)),
            pl.BlockSpec(memory_space=pl.ANY),
            pl.BlockSpec((tm, d), lambda i: (i, 0)),
            pl.BlockSpec((1, N_MOD, d), lambda i: ((i * tm) // seq, 0, 0)),
            pl.BlockSpec((1, d), lambda i: (0, 0)),
        ],
        out_specs=pl.BlockSpec((tm, d), lambda i: (i, 0)),
        out_shape=jax.ShapeDtypeStruct((n_tok, d), F32),
        scratch_shapes=[pltpu.VMEM((TOP_K, tm // DMA_UNROLL, DMA_UNROLL, d), F32), pltpu.SemaphoreType.DMA],
        compiler_params=pltpu.CompilerParams(dimension_semantics=("arbitrary",), vmem_limit_bytes=VMEM_LIMIT),
        name="combine",
    )(dest, wt.T, ys, x1, mod3, final_g.reshape(1, d))


def _layer(x2, mod3, bsz, seq, norm1_g, w_in, conv_w, conv_b, ml_b_i, ml_b_f, ml_norm_g, w_branch_a, w_branch_b,
           w_out, norm2_g, router_w, router_b, expert_w1, expert_b1, expert_w2, expert_b2):
    d = D_MODEL
    n_tok = bsz * seq
    gate0 = 7 * d
    n_gate = 2 * ML_HEADS
    w_main = jnp.concatenate([w_in[:, :gate0], w_in[:, gate0 + n_gate:]], axis=1).astype(BF16)
    w_gate = jnp.pad(w_in[:, gate0:gate0 + n_gate], ((0, 0), (0, LANES - n_gate))).astype(BF16)
    gate_bias = jnp.pad(jnp.concatenate([ml_b_i, ml_b_f]), (0, LANES - n_gate)).reshape(1, LANES)

    proj, gates = _proj(x2, mod3, norm1_g, w_main, w_gate, seq)

    ones = jnp.ones((LANES, LANES), F32)
    uu = jnp.concatenate([jnp.tril(ones), ones], axis=1).astype(BF16)
    ya = _sb_attention(proj, uu, bsz, seq)

    cl = min(ML_CHUNK, seq)
    tri = jnp.tril(jnp.ones((cl, cl), F32))
    yb = _mlstm(proj, gates, gate_bias, conv_w, conv_b, ml_norm_g, tri, bsz, seq)

    ts = min(ROW_TILE_MERGE, seq) // MERGE_SUB
    su = jnp.triu(jnp.ones((ts, ts), F32), k=1).astype(BF16)
    x1, h2s, idx, wt, rank, cnt = _merge(
        ya, yb, proj, x2, mod3, w_branch_a.astype(BF16), w_branch_b.astype(BF16), w_out.astype(BF16),
        norm2_g, router_w.T, router_b, su, seq)

    bm = ROW_TILE_MOE
    counts = cnt[:, 0].astype(I32)
    padded = (counts + bm - 1) // bm * bm
    pad_end = jnp.cumsum(padded)
    start = pad_end - padded
    n_rows = (n_tok * TOP_K + N_EXPERTS * (bm - 1)) // bm * bm
    n_blocks = n_rows // bm
    n_used = jnp.maximum(pad_end[-1] // bm, 1).astype(I32)
    blk = jnp.minimum(jnp.arange(n_blocks, dtype=I32), n_used - 1)
    block_e = jnp.minimum(jnp.sum((pad_end[None, :] <= (blk * bm)[:, None]).astype(I32), axis=1), N_EXPERTS - 1)

    onehot = idx[:, :, None] == jnp.arange(N_EXPERTS, dtype=I32)
    dest = rank + jnp.sum(jnp.where(onehot, start, 0), axis=-1)

    xs = _scatter(start, pad_end, n_used.reshape(1), dest, h2s, n_rows, seq)
    ys = _experts(block_e, n_used.reshape(1), xs, expert_w1, expert_b1, expert_w2, expert_b2)
    return dest, wt, ys, x1


def kernel(x, c, ada_w, ada_b, norm1_g, w_in, conv_w, conv_b, ml_b_i, ml_b_f, ml_norm_g, w_branch_a, w_branch_b, w_out, norm2_g, router_w, router_b, expert_w1, expert_b1, expert_w2, expert_b2, final_g):
    bsz, seq, d = x.shape
    depth = ada_w.shape[0]
    assert d == D_MODEL and depth == 1 and seq % QUERY_TILE == 0
    x2 = x.reshape(bsz * seq, d)
    mod3 = _ada(c, ada_w[0], ada_b[0]).reshape(bsz, N_MOD, d)
    dest, wt, ys, x1 = _layer(
        x2, mod3, bsz, seq, norm1_g[0], w_in[0], conv_w[0], conv_b[0], ml_b_i[0], ml_b_f[0], ml_norm_g[0],
        w_branch_a[0], w_branch_b[0], w_out[0], norm2_g[0], router_w[0], router_b[0],
        expert_w1[0], expert_b1[0], expert_w2[0], expert_b2[0])
    out = _combine(dest, wt, ys, x1, mod3, final_g, seq)
    return out.reshape(bsz, seq, d)
```
